```python
import functools
import jax, jax.numpy as jnp
from jax import lax
import numpy as np

D_MODEL = 1024
BATCH = 2
SEQ = 8192
DEPTH = 4
DEC_BATCH = 32
DEC_SEQ = 16
PAST_LEN = 2048

CHUNK = 64
CONV_W = 31
CA = 512
HB = 8
DHB = 64
DB = HB * DHB
HC = 4
DK = 128
DV = 128
DC = HC * DV
D_FF = 4 * D_MODEL
QBLK = 128
EPS = 1e-6
IN_WIDTH = 2 * CA + 3 * DB + HB + 2 * HC * DK + 2 * DC + 3 * D_MODEL

kernel_name = 'hybrid_streaming_conv_fox_hgrn2'


def rms_norm(x, g):
    xf = x.astype(jnp.float32)
    y = xf * lax.rsqrt(jnp.mean(xf * xf, axis=-1, keepdims=True) + EPS)
    return (y * g.astype(jnp.float32)).astype(x.dtype)


def layer_norm(x, g, b):
    xf = x.astype(jnp.float32)
    mu = jnp.mean(xf, axis=-1, keepdims=True)
    xc = xf - mu
    y = xc * lax.rsqrt(jnp.mean(xc * xc, axis=-1, keepdims=True) + EPS)
    return (y * g.astype(jnp.float32) + b.astype(jnp.float32)).astype(x.dtype)


def split_points():
    widths = [CA, CA, DB, DB, DB, HB, HC * DK, HC * DK, DC, DC, D_MODEL, D_MODEL, D_MODEL]
    return [int(v) for v in np.cumsum(widths)[:-1]]


def conformer_conv(a_v, a_g, hist, conv_w, conv_b, ln_g, ln_b, w_out):
    u = a_v * jax.nn.sigmoid(a_g)
    upad = jnp.concatenate([hist.astype(u.dtype), u], axis=1)
    y = lax.conv_general_dilated(upad, conv_w[:, None, :].astype(u.dtype), window_strides=(1,),
                                 padding='VALID', dimension_numbers=('NWC', 'WIO', 'NWC'),
                                 feature_group_count=CA) + conv_b
    y = jax.nn.silu(layer_norm(y, ln_g, ln_b))
    return y @ w_out, upad[:, -(CONV_W - 1):]


def fox_attend_prompt(q, k, v, logf):
    B, S, H, Dh = q.shape
    nb = S // QBLK
    c = jnp.cumsum(logf.astype(jnp.float32), axis=1).swapaxes(1, 2)
    qb = q.reshape(B, nb, QBLK, H, Dh).swapaxes(0, 1)
    cb = c.reshape(B, H, nb, QBLK).transpose(2, 0, 1, 3)
    kpos = jnp.arange(S)
    scale = DHB ** -0.5

    def block(args):
        qi, ci, bi = args
        s = jnp.einsum('bqhd,bkhd->bhqk', qi, k).astype(jnp.float32) * scale
        s = s + ci[..., :, None] - c[..., None, :]
        qpos = bi * QBLK + jnp.arange(QBLK)
        s = jnp.where(kpos[None, :] <= qpos[:, None], s, -jnp.inf)
        p = jax.nn.softmax(s, axis=-1)
        return jnp.einsum('bhqk,bkhd->bqhd', p.astype(v.dtype), v)

    o = lax.map(block, (qb, cb, jnp.arange(nb)))
    return o.swapaxes(0, 1).reshape(B, S, H, Dh)


def fox_attend_cached(q, k, v, logf, k_cache, v_cache, lf_cache):
    P = k_cache.shape[1]
    L = q.shape[1]
    k_all = jnp.concatenate([k_cache.astype(k.dtype), k], axis=1)
    v_all = jnp.concatenate([v_cache.astype(v.dtype), v], axis=1)
    c = jnp.cumsum(jnp.concatenate([lf_cache.astype(jnp.float32), logf.astype(jnp.float32)], axis=1),
                   axis=1).swapaxes(1, 2)
    s = jnp.einsum('bqhd,bkhd->bhqk', q, k_all).astype(jnp.float32) * (DHB ** -0.5)
    s = s + c[:, :, P:, None] - c[:, :, None, :]
    mask = jnp.arange(P + L)[None, :] <= (P + jnp.arange(L))[:, None]
    s = jnp.where(mask, s, -jnp.inf)
    p = jax.nn.softmax(s, axis=-1)
    return jnp.einsum('bhqk,bkhd->bqhd', p.astype(v.dtype), v_all)


def hgrn2_scan(q, kk, v, logf, S0, chunk):
    B, L, H, K = q.shape
    V = v.shape[-1]
    n = L // chunk

    def split(t):
        return t.reshape(B, n, chunk, *t.shape[2:]).swapaxes(0, 1)

    tri = jnp.tril(jnp.ones((chunk, chunk), dtype=bool))

    def step(S, xs):
        qc, kc, vc, gc = xs
        b = jnp.cumsum(gc, axis=1)
        diff = b[:, :, None] - b[:, None, :]
        dec = jnp.exp(jnp.where(tri[None, :, :, None, None], diff, -jnp.inf))
        A = jnp.einsum('bthk,btshk,bshk->bhts', qc, dec, kc)
        o = jnp.einsum('bhts,bshv->bthv', A, vc) + jnp.einsum('bthk,bhkv->bthv', qc * jnp.exp(b), S)
        bl = b[:, -1]
        S = jnp.exp(bl)[..., None] * S + jnp.einsum('bshk,bshv->bhkv', kc * jnp.exp(bl[:, None] - b), vc)
        return S, o

    S, o = lax.scan(step, S0, (split(q), split(kk), split(v), split(logf)))
    return o.swapaxes(0, 1).reshape(B, L, H, V), S


def hgrn2_lower_bounds(p):
    s = jax.nn.softmax(p.astype(jnp.float32), axis=0)
    return jnp.maximum(jnp.cumsum(s, axis=0) - s[0], 0.0)


def hgrn2_branch(q_c, f_c, i_c, o_c, lb, S0, norm_g, w_out):
    B, L, _ = q_c.shape
    f32 = jnp.float32
    q = jax.nn.silu(q_c.astype(f32)).reshape(B, L, HC, DK)
    zf = f_c.astype(f32).reshape(B, L, HC, DK)
    lbh = lb.reshape(HC, DK)
    logf = jnp.logaddexp(jnp.log(lbh), jnp.log1p(-lbh) + jax.nn.log_sigmoid(zf))
    kk = (1.0 - lbh) * jax.nn.sigmoid(-zf)
    v = i_c.astype(f32).reshape(B, L, HC, DV)
    o, S = hgrn2_scan(q, kk, v, logf, S0.astype(f32), min(CHUNK, L))
    o = o * lax.rsqrt(jnp.mean(o * o, axis=-1, keepdims=True) + EPS) * norm_g.astype(f32)
    o = o * jax.nn.silu(o_c.astype(f32).reshape(B, L, HC, DV))
    return o.reshape(B, L, DC).astype(q_c.dtype) @ w_out, S


def trunk_layer(x, attend, conv_hist, S0, lb, g1, w_in, conv_w, conv_b, ln_g, ln_b, w_a_out,
                fox_bf, w_b_out, hn_g, w_c_out, w_o, g2, w_up, w_down):
    B, L, _ = x.shape
    h = rms_norm(x, g1)
    z = h @ w_in
    a_v, a_g, q_b, k_b, v_b, f_b, q_c, f_c, i_c, o_c, g_a, g_b, g_c = jnp.split(z, split_points(), axis=-1)
    y_a, conv_new = conformer_conv(a_v, a_g, conv_hist, conv_w, conv_b, ln_g, ln_b, w_a_out)
    qb = q_b.reshape(B, L, HB, DHB)
    kb = k_b.reshape(B, L, HB, DHB)
    vb = v_b.reshape(B, L, HB, DHB)
    logf_b = jax.nn.log_sigmoid((f_b + fox_bf).astype(jnp.float32))
    o_b = attend(qb, kb, vb, logf_b)
    y_b = o_b.reshape(B, L, DB) @ w_b_out
    y_c, S_new = hgrn2_branch(q_c, f_c, i_c, o_c, lb, S0, hn_g, w_c_out)
    m = jax.nn.sigmoid(g_a) * y_a + jax.nn.sigmoid(g_b) * y_b + jax.nn.sigmoid(g_c) * y_c
    x = x + m @ w_o
    u = rms_norm(x, g2) @ w_up
    x = x + jnp.square(jax.nn.relu(u)) @ w_down
    return x, kb, vb, logf_b, conv_new, S_new


def setup_inputs(seed: int = 0) -> dict:
    key = jax.random.key(seed)
    ks = jax.random.split(key, 32)

    def nrm(k, shape, scale):
        return jax.random.normal(k, shape, jnp.float32) * scale

    return {
        'x_prompt': nrm(ks[0], (BATCH, SEQ, D_MODEL), 1.0),
        'x_sample': nrm(ks[1], (DEC_BATCH, DEC_SEQ, D_MODEL), 1.0),
        'cache_fox_k': nrm(ks[2], (DEPTH, DEC_BATCH, PAST_LEN, HB, DHB), 1.0),
        'cache_fox_v': nrm(ks[3], (DEPTH, DEC_BATCH, PAST_LEN, HB, DHB), 1.0),
        'cache_fox_logf': jax.nn.log_sigmoid(2.0 + nrm(ks[4], (DEPTH, DEC_BATCH, PAST_LEN, HB), 1.0)),
        'state_conv': nrm(ks[5], (DEPTH, DEC_BATCH, CONV_W - 1, CA), 0.5),
        'state_hgrn': nrm(ks[6], (DEPTH, DEC_BATCH, HC, DK, DV), 0.5),
        'norm1_g': 1.0 + nrm(ks[7], (DEPTH, D_MODEL), 0.02),
        'w_in': nrm(ks[8], (DEPTH, D_MODEL, IN_WIDTH), D_MODEL ** -0.5),
        'conv_w': nrm(ks[9], (DEPTH, CONV_W, CA), CONV_W ** -0.5),
        'conv_b': nrm(ks[10], (DEPTH, CA), 0.02),
        'conv_ln_g': 1.0 + nrm(ks[11], (DEPTH, CA), 0.02),
        'conv_ln_b': nrm(ks[12], (DEPTH, CA), 0.02),
        'w_a_out': nrm(ks[13], (DEPTH, CA, D_MODEL), CA ** -0.5),
        'fox_bf': 2.0 + nrm(ks[14], (DEPTH, HB), 0.5),
        'w_b_out': nrm(ks[15], (DEPTH, DB, D_MODEL), DB ** -0.5),
        'hgrn_lb_param': nrm(ks[16], (DEPTH, HC * DK), 0.5),
        'hgrn_norm_g': 1.0 + nrm(ks[17], (DEPTH, DV), 0.02),
        'w_c_out': nrm(ks[18], (DEPTH, DC, D_MODEL), DC ** -0.5),
        'w_o': nrm(ks[19], (DEPTH, D_MODEL, D_MODEL), D_MODEL ** -0.5),
        'norm2_g': 1.0 + nrm(ks[20], (DEPTH, D_MODEL), 0.02),
        'w_up': nrm(ks[21], (DEPTH, D_MODEL, D_FF), D_MODEL ** -0.5),
        'w_down': nrm(ks[22], (DEPTH, D_FF, D_MODEL), D_FF ** -0.5),
        'final_g': 1.0 + nrm(ks[23], (D_MODEL,), 0.02),
    }


def reference(x_prompt, x_sample, cache_fox_k, cache_fox_v, cache_fox_logf, state_conv, state_hgrn,
              norm1_g, w_in, conv_w, conv_b, conv_ln_g, conv_ln_b, w_a_out, fox_bf, w_b_out,
              hgrn_lb_param, hgrn_norm_g, w_c_out, w_o, norm2_g, w_up, w_down, final_g):
    lbs = hgrn2_lower_bounds(hgrn_lb_param)
    xp, xs = x_prompt, x_sample
    Bp = xp.shape[0]
    kp_l, vp_l, lfp_l, cp_l, hp_l = [], [], [], [], []
    ks_l, vs_l, lfs_l, cs_l, hs_l = [], [], [], [], []
    for l in range(DEPTH):
        w = (lbs[l], norm1_g[l], w_in[l], conv_w[l], conv_b[l], conv_ln_g[l], conv_ln_b[l], w_a_out[l],
             fox_bf[l], w_b_out[l], hgrn_norm_g[l], w_c_out[l], w_o[l], norm2_g[l], w_up[l], w_down[l])
        zero_hist = jnp.zeros((Bp, CONV_W - 1, CA), xp.dtype)
        zero_S = jnp.zeros((Bp, HC, DK, DV), jnp.float32)
        xp, kp, vp, lfp, cp, hp = trunk_layer(xp, fox_attend_prompt, zero_hist, zero_S, *w)
        attend_s = functools.partial(fox_attend_cached, k_cache=cache_fox_k[l], v_cache=cache_fox_v[l],
                                     lf_cache=cache_fox_logf[l])
        xs, k_s, v_s, lfs, cs, hs = trunk_layer(xs, attend_s, state_conv[l], state_hgrn[l], *w)
        kp_l.append(kp); vp_l.append(vp); lfp_l.append(lfp); cp_l.append(cp); hp_l.append(hp)
        ks_l.append(k_s); vs_l.append(v_s); lfs_l.append(lfs); cs_l.append(cs); hs_l.append(hs)
    y_prompt = rms_norm(xp, final_g)
    y_sample = rms_norm(xs, final_g)
    return (y_prompt, y_sample,
            jnp.stack(kp_l), jnp.stack(vp_l), jnp.stack(lfp_l), jnp.stack(cp_l), jnp.stack(hp_l),
            jnp.stack(ks_l), jnp.stack(vs_l), jnp.stack(lfs_l), jnp.stack(cs_l), jnp.stack(hs_l))
```

```python
import functools
import math

import numpy as np
import jax
import jax.numpy as jnp
from jax import lax
from jax.experimental import pallas as pl
from jax.experimental.pallas import tpu as pltpu

D_MODEL = 1024
CONV_W = 31
CA = 512
HB = 8
DHB = 64
DB = HB * DHB
HC = 4
DK = 128
DV = 128
DC = HC * DV
D_FF = 4 * D_MODEL
EPS = 1e-6
HGRN_CHUNK = 64

LANES = 128
VMEM_LIMIT = 56 * 1024 * 1024
NEG_BIG = -1e30

F32 = jnp.float32
BF16 = jnp.bfloat16

_OFF = np.cumsum([0, CA, CA, DB, DB, DB, HB, HC * DK, HC * DK, DC, DC, D_MODEL, D_MODEL, D_MODEL])
(O_AV, O_AG, O_QB, O_KB, O_VB, O_FB, O_QC, O_FC, O_IC, O_OC, O_GA, O_GB, O_GC, O_END) = [int(v) for v in _OFF]


def _dot(a, b):
    return jnp.dot(a, b, preferred_element_type=F32)


def _dot_nt(a, b):
    return lax.dot_general(a, b, (((1,), (1,)), ((), ())), preferred_element_type=F32)


def _dot_tn(a, b):
    return lax.dot_general(a, b, (((0,), (0,)), ((), ())), preferred_element_type=F32)


def _sigmoid(x):
    return 1.0 / (1.0 + jnp.exp(-x))


def _silu(x):
    return x * _sigmoid(x)


def _rms(x, g):
    return x * lax.rsqrt(jnp.mean(x * x, axis=-1, keepdims=True) + EPS) * g


def _split3(x):
    hi = x.astype(BF16)
    r = x - hi.astype(F32)
    mid = r.astype(BF16)
    lo = (r - mid.astype(F32)).astype(BF16)
    return hi, mid, lo


def _const_spec(shape):
    nd = len(shape)
    return pl.BlockSpec(shape, lambda *_: (0,) * nd, pipeline_mode=pl.Buffered(1))


A_WIDTH = 9 * 512 + LANES


def _in_proj_kernel(x_ref, g1_ref, w_ref, fbias_ref, lb_ref,
                    u_ref, qs_ref, k_ref, v_ref, lf_ref, qh_ref, lfc_ref, kk_ref, vv_ref, og_ref):
    x = x_ref[...]
    h = _rms(x, g1_ref[...]).astype(BF16)

    def seg(i, width=512):
        return _dot(h, w_ref[:, i * 512:i * 512 + width])

    u_ref[...] = seg(0) * _sigmoid(seg(1))
    qs_ref[...] = (seg(2) * (DHB ** -0.5)).astype(BF16)
    k_ref[...] = seg(3)
    v_ref[...] = seg(4)
    qh_ref[...] = _silu(seg(5)).astype(BF16)

    z = seg(6)
    log_lb = lb_ref[0:1, :]
    log1m_lb = lb_ref[1:2, :]
    one_m_lb = lb_ref[2:3, :]
    e = jnp.exp(-jnp.abs(z))
    ls = jnp.minimum(z, 0.0) - jnp.log(1.0 + e)
    b = log1m_lb + ls
    mx = jnp.maximum(log_lb, b)
    lfc_ref[...] = mx + jnp.log(1.0 + jnp.exp(-jnp.abs(log_lb - b)))
    r = 1.0 / (1.0 + e)
    kk_ref[...] = (one_m_lb * jnp.where(z >= 0.0, e * r, r)).astype(BF16)

    vv_ref[...] = seg(7).astype(BF16)
    og_ref[...] = _silu(seg(8)).astype(BF16)

    f = seg(9, LANES) + fbias_ref[...]
    lf = jnp.minimum(f, 0.0) - jnp.log(1.0 + jnp.exp(-jnp.abs(f)))
    lf_ref[...] = lf[:, :HB]


def _in_proj(x, g1, w, fbias, lbrows, tm):
    n = x.shape[0]
    tok = lambda w_, dt: jax.ShapeDtypeStruct((n, w_), dt)
    row = lambda w_: pl.BlockSpec((tm, w_), lambda i: (i, 0))
    out_shape = (tok(512, F32), tok(512, BF16), tok(512, F32), tok(512, F32), tok(HB, F32),
                 tok(512, BF16), tok(512, F32), tok(512, BF16), tok(512, BF16), tok(512, BF16))
    out_specs = (row(512), row(512), row(512), row(512), row(HB),
                 row(512), row(512), row(512), row(512), row(512))
    return pl.pallas_call(
        _in_proj_kernel,
        grid=(n // tm,),
        in_specs=[row(D_MODEL), _const_spec((1, D_MODEL)), _const_spec((D_MODEL, A_WIDTH)),
                  _const_spec((1, LANES)), _const_spec((8, 512))],
        out_specs=out_specs,
        out_shape=out_shape,
        compiler_params=pltpu.CompilerParams(dimension_semantics=("parallel",),
                                             vmem_limit_bytes=VMEM_LIMIT),
        name="in_proj",
    )(x, g1, w, fbias, lbrows)


def _cumsum_kernel(lf_ref, hi_ref, mid_ref, lo_ref, *, n, pivot):
    x = lf_ref[0]
    lane = lax.broadcasted_iota(jnp.int32, x.shape, 1)
    s = 1
    while s < n:
        x = x + jnp.where(lane >= s, pltpu.roll(x, s, 1), 0.0)
        s *= 2
    if pivot is None:
        bias = -x
    else:
        piv = jnp.sum(jnp.where(lane == pivot, x, 0.0), axis=1, keepdims=True)
        bias = piv - x
    hi, mid, lo = _split3(bias)
    hi_ref[0] = hi
    mid_ref[0] = mid
    lo_ref[0] = lo


def _cumsum_bias(lf_t, pivot=None):
    bsz, hh, n = lf_t.shape
    spec = pl.BlockSpec((1, hh, n), lambda b: (b, 0, 0))
    sds = jax.ShapeDtypeStruct((bsz, hh, n), BF16)
    return pl.pallas_call(
        functools.partial(_cumsum_kernel, n=n, pivot=pivot),
        grid=(bsz,),
        in_specs=[spec],
        out_specs=(spec, spec, spec),
        out_shape=(sds, sds, sds),
        compiler_params=pltpu.CompilerParams(dimension_semantics=("parallel",)),
        name="cumsum_bias",
    )(lf_t)


HIST_ROWS = 32
HIST_PAD = HIST_ROWS - (CONV_W - 1)


def _conv_kernel(u_ref, hist_ref, w_ref, cb_ref, g_ref, b_ref, feat_ref, new_ref, buf_ref, *, tl, rc):
    t = pl.program_id(1)

    @pl.when(t == 0)
    def _():
        buf_ref[0:HIST_ROWS, :] = hist_ref[0]

    @pl.when(t > 0)
    def _():
        buf_ref[0:HIST_ROWS, :] = buf_ref[tl:tl + HIST_ROWS, :]

    buf_ref[HIST_ROWS:HIST_ROWS + tl, :] = u_ref[0]
    new_ref[0] = buf_ref[tl:tl + HIST_ROWS, :]

    cb = cb_ref[...]
    g = g_ref[...]
    b = b_ref[...]
    for r0 in range(0, tl, rc):
        acc = jnp.zeros((rc, CA), F32)
        for j in range(CONV_W):
            acc = acc + w_ref[j:j + 1, :] * buf_ref[r0 + HIST_PAD + j:r0 + HIST_PAD + j + rc, :]
        y = acc + cb
        mu = jnp.mean(y, axis=-1, keepdims=True)
        yc = y - mu
        yn = yc * lax.rsqrt(jnp.mean(yc * yc, axis=-1, keepdims=True) + EPS) * g + b
        feat_ref[0, r0:r0 + rc, :] = _silu(yn).astype(BF16)


def _conv(u, hist, w, cb, g, b, tl):
    bsz, L, _ = u.shape
    rc = min(tl, 32)
    vec = _const_spec((1, CA))
    return pl.pallas_call(
        functools.partial(_conv_kernel, tl=tl, rc=rc),
        grid=(bsz, L // tl),
        in_specs=[pl.BlockSpec((1, tl, CA), lambda i, t: (i, t, 0)),
                  pl.BlockSpec((1, HIST_ROWS, CA), lambda i, t: (i, 0, 0)),
                  _const_spec((HIST_ROWS, CA)), vec, vec, vec],
        out_specs=(pl.BlockSpec((1, tl, CA), lambda i, t: (i, t, 0)),
                   pl.BlockSpec((1, HIST_ROWS, CA), lambda i, t: (i, 0, 0))),
        out_shape=(jax.ShapeDtypeStruct((bsz, L, CA), BF16),
                   jax.ShapeDtypeStruct((bsz, HIST_ROWS, CA), F32)),
        scratch_shapes=[pltpu.VMEM((tl + HIST_ROWS, CA), F32)],
        compiler_params=pltpu.CompilerParams(dimension_semantics=("parallel", "arbitrary")),
        name="conv",
    )(u, hist, w, cb, g, b)


ONES_ROWS = 16


def _attn_kernel(q_ref, k_ref, vt_ref, o_ref, *, tq):
    i = pl.program_id(2)
    q = q_ref[...]
    ones = jnp.ones((ONES_ROWS, tq), BF16)

    def step(j, m, acc, diag):
        k = k_ref[pl.ds(pl.multiple_of(j * tq, tq), tq), :]
        s = _dot_nt(k, q)
        if diag:
            kk = lax.broadcasted_iota(jnp.int32, (tq, tq), 0)
            qq = lax.broadcasted_iota(jnp.int32, (tq, tq), 1)
            s = jnp.where(kk <= qq, s, NEG_BIG)
        m_new = jnp.maximum(m, jnp.max(s, axis=0, keepdims=True))
        alpha = jnp.exp(m - m_new)
        p = jnp.exp(s - m_new).astype(BF16)
        vt = jnp.concatenate([vt_ref[j], ones], axis=0)
        return m_new, acc * alpha + _dot(vt, p)

    m0 = jnp.full((1, tq), NEG_BIG, F32)
    acc0 = jnp.zeros((DHB + ONES_ROWS, tq), F32)
    m, acc = lax.fori_loop(0, i, lambda j, c: step(j, c[0], c[1], False), (m0, acc0))
    m, acc = step(i, m, acc, True)
    o_ref[...] = (acc[:DHB] / acc[DHB:DHB + 1]).astype(BF16)


def _attn_prompt(q_aug, k_aug, vt_blk, tq):
    bsz, hh, S, _ = q_aug.shape
    nk = S // tq
    return pl.pallas_call(
        functools.partial(_attn_kernel, tq=tq),
        grid=(bsz, hh, S // tq),
        in_specs=[pl.BlockSpec((None, None, tq, LANES), lambda b, h, i: (b, h, i, 0)),
                  pl.BlockSpec((None, None, S, LANES), lambda b, h, i: (b, h, 0, 0)),
                  pl.BlockSpec((None, None, nk, DHB, tq), lambda b, h, i: (b, h, 0, 0, 0))],
        out_specs=pl.BlockSpec((None, DHB, tq), lambda b, h, i: (b, h, i)),
        out_shape=jax.ShapeDtypeStruct((bsz, DB, S), BF16),
        compiler_params=pltpu.CompilerParams(dimension_semantics=("parallel", "parallel", "arbitrary"),
                                             vmem_limit_bytes=VMEM_LIMIT),
        name="attn_prompt",
    )(q_aug, k_aug, vt_blk)


def _attn_cached_kernel(q_ref, kc_ref, vc_ref, kn_ref, vn_ref, bhi_ref, bmid_ref, blo_ref, o_ref, *, L, P):
    rows = HB * L
    q = q_ref[0]
    lane_head = lax.broadcasted_iota(jnp.int32, (L, DB), 1) // DHB
    q_bd = jnp.concatenate([jnp.where(lane_head == h, q, jnp.zeros_like(q)) for h in range(HB)], axis=0)
    kc = kc_ref[0].astype(BF16)
    vc = vc_ref[0].astype(BF16)
    zpad = jnp.zeros((LANES - L, DB), BF16)
    kn = jnp.concatenate([kn_ref[0].astype(BF16), zpad], axis=0)
    vn = jnp.concatenate([vn_ref[0].astype(BF16), zpad], axis=0)
    bias = (bhi_ref[0].astype(F32) + bmid_ref[0].astype(F32)) + blo_ref[0].astype(F32)

    s_c = _dot_nt(q_bd, kc)
    s_n = _dot_nt(q_bd, kn)
    tt = lax.broadcasted_iota(jnp.int32, (L, LANES), 0)
    uu = lax.broadcasted_iota(jnp.int32, (L, LANES), 1)
    causal = uu <= tt
    pcs, pns, ls = [], [], []
    for h in range(HB):
        sc = s_c[h * L:(h + 1) * L, :] + bias[h:h + 1, :P]
        sn = jnp.where(causal, s_n[h * L:(h + 1) * L, :] + bias[h:h + 1, P:], NEG_BIG)
        m = jnp.maximum(jnp.max(sc, axis=1, keepdims=True), jnp.max(sn, axis=1, keepdims=True))
        pc = jnp.exp(sc - m)
        pn = jnp.exp(sn - m)
        ls.append(jnp.sum(pc, axis=1, keepdims=True) + jnp.sum(pn, axis=1, keepdims=True))
        pcs.append(pc.astype(BF16))
        pns.append(pn.astype(BF16))
    o_all = _dot(jnp.concatenate(pcs, axis=0), vc) + _dot(jnp.concatenate(pns, axis=0), vn)
    out = jnp.zeros((L, DB), F32)
    for h in range(HB):
        out = out + jnp.where(lane_head == h, o_all[h * L:(h + 1) * L, :] / ls[h], 0.0)
    o_ref[0] = out.astype(BF16)
    del rows


def _attn_cached(qs, k_cache, v_cache, k_new, v_new, bhi, bmid, blo):
    bsz, L, _ = qs.shape
    P = k_cache.shape[1]
    seq = lambda r, w: pl.BlockSpec((1, r, w), lambda b: (b, 0, 0))
    return pl.pallas_call(
        functools.partial(_attn_cached_kernel, L=L, P=P),
        grid=(bsz,),
        in_specs=[seq(L, DB), seq(P, DB), seq(P, DB), seq(L, DB), seq(L, DB),
                  seq(HB, P + LANES), seq(HB, P + LANES), seq(HB, P + LANES)],
        out_specs=seq(L, DB),
        out_shape=jax.ShapeDtypeStruct((bsz, L, DB), BF16),
        compiler_params=pltpu.CompilerParams(dimension_semantics=("parallel",),
                                             vmem_limit_bytes=VMEM_LIMIT),
        name="attn_cached",
    )(qs, k_cache, v_cache, k_new, v_new, bhi, bmid, blo)


def _hgrn_exponent_matrix(C):
    nl = int(math.log2(C))
    rows = []
    idx = np.arange(C)
    for lv in range(nl):
        sz = 1 << lv
        m = np.zeros((C, C), np.float32)
        for t in range(C):
            bnd = ((t >> (lv + 1)) << (lv + 1)) + sz - 1
            if (t >> lv) & 1:
                m[t, (idx > bnd) & (idx <= t)] = 1.0
            else:
                m[t, (idx > t) & (idx <= bnd)] = 1.0
        rows.append(m)
    rows.append((idx[None, :] <= idx[:, None]).astype(np.float32))
    rows.append((idx[None, :] > idx[:, None]).astype(np.float32))
    p = np.concatenate(rows, axis=0)
    return np.concatenate([p, p, p], axis=1)


def _hgrn_kernel(p_ref, qh_ref, lf_ref, kk_ref, vv_ref, og_ref, s0_ref, g_ref, hg_ref, sout_ref, st_ref,
                 *, C, T):
    t = pl.program_id(1)
    nl = int(math.log2(C))

    @pl.when(t == 0)
    def _():
        for h in range(HC):
            st_ref[h] = s0_ref[0, h].T

    row = lax.broadcasted_iota(jnp.int32, (C, C), 0)
    col = lax.broadcasted_iota(jnp.int32, (C, C), 1)
    diff = row ^ col
    masks = [(jnp.right_shift(diff, lv) == 1) & ((jnp.right_shift(row, lv) & 1) == 1) for lv in range(nl)]
    diag = row == col
    g = g_ref[...]
    pmat = p_ref[...]

    for c0 in range(0, T, C):
        hi, mid, lo = _split3(lf_ref[0, c0:c0 + C, :])
        e_all = jnp.exp(_dot(pmat, jnp.concatenate([hi, mid, lo], axis=0)))
        for h in range(HC):
            hs = slice(h * DK, (h + 1) * DK)
            q = qh_ref[0, c0:c0 + C, hs].astype(F32)
            k = kk_ref[0, c0:c0 + C, hs].astype(F32)
            v = vv_ref[0, c0:c0 + C, hs]
            a = jnp.where(diag, _dot_nt(q.astype(BF16), k.astype(BF16)), 0.0)
            for lv in range(nl):
                e = e_all[lv * C:(lv + 1) * C, hs]
                a = a + jnp.where(masks[lv], _dot_nt((q * e).astype(BF16), (k * e).astype(BF16)), 0.0)
            e_q = e_all[nl * C:(nl + 1) * C, hs]
            e_k = e_all[(nl + 1) * C:(nl + 2) * C, hs]
            st = st_ref[h]
            o = _dot(a.astype(BF16), v) + _dot_nt((q * e_q).astype(BF16), st.astype(BF16))
            st_ref[h] = st * e_q[C - 1:C, :] + _dot_tn(v, (k * e_k).astype(BF16))
            o = o * lax.rsqrt(jnp.mean(o * o, axis=-1, keepdims=True) + EPS) * g
            hg_ref[0, c0:c0 + C, hs] = (o * og_ref[0, c0:c0 + C, hs].astype(F32)).astype(BF16)

    @pl.when(t == pl.num_programs(1) - 1)
    def _():
        for h in range(HC):
            sout_ref[0, h] = st_ref[h].T


def _hgrn(qh, lfc, kk, vv, og, s0, g, C, T):
    bsz, L, _ = qh.shape
    pmat = jnp.asarray(_hgrn_exponent_matrix(C), BF16)
    tile = pl.BlockSpec((1, T, DC), lambda b, t: (b, t, 0))
    state = pl.BlockSpec((1, HC, DK, DV), lambda b, t: (b, 0, 0, 0))
    return pl.pallas_call(
        functools.partial(_hgrn_kernel, C=C, T=T),
        grid=(bsz, L // T),
        in_specs=[_const_spec(pmat.shape), tile, tile, tile, tile, tile, state, _const_spec((1, DV))],
        out_specs=(tile, state),
        out_shape=(jax.ShapeDtypeStruct((bsz, L, DC), BF16),
                   jax.ShapeDtypeStruct((bsz, HC, DK, DV), F32)),
        scratch_shapes=[pltpu.VMEM((HC, DV, DK), F32)],
        compiler_params=pltpu.CompilerParams(dimension_semantics=("parallel", "arbitrary")),
        name="hgrn",
    )(pmat, qh, lfc, kk, vv, og, s0, g)


FF_CHUNK = 1024


def _merge_ffn_kernel(x_ref, fa_ref, fb_ref, fc_ref, g1_ref, wg_ref, wa_ref, wb_ref, wc_ref, wo_ref,
                      g2_ref, wup_ref, wdn_ref, gf_ref, o_ref, *, final):
    x = x_ref[...]
    h1 = _rms(x, g1_ref[...]).astype(BF16)
    m = None
    for i, (f_ref, w_ref) in enumerate(((fa_ref, wa_ref), (fb_ref, wb_ref), (fc_ref, wc_ref))):
        gate = _sigmoid(_dot(h1, wg_ref[:, i * D_MODEL:(i + 1) * D_MODEL]))
        y = gate * _dot(f_ref[...], w_ref[...])
        m = y if m is None else m + y
    x = x + _dot(m.astype(BF16), wo_ref[...])
    h2 = _rms(x, g2_ref[...]).astype(BF16)
    acc = None
    for c in range(0, D_FF, FF_CHUNK):
        up = jnp.maximum(_dot(h2, wup_ref[:, c:c + FF_CHUNK]), 0.0)
        d = _dot((up * up).astype(BF16), wdn_ref[c:c + FF_CHUNK, :])
        acc = d if acc is None else acc + d
    x = x + acc
    if final:
        x = _rms(x, gf_ref[...])
    o_ref[...] = x


def _merge_ffn(x, fa, fb, fc, w, final_g, tm, final):
    n = x.shape[0]
    row = lambda w_: pl.BlockSpec((tm, w_), lambda i: (i, 0))
    return pl.pallas_call(
        functools.partial(_merge_ffn_kernel, final=final),
        grid=(n // tm,),
        in_specs=[row(D_MODEL), row(CA), row(DB), row(DC),
                  _const_spec((1, D_MODEL)), _const_spec((D_MODEL, 3 * D_MODEL)),
                  _const_spec((CA, D_MODEL)), _const_spec((DB, D_MODEL)), _const_spec((DC, D_MODEL)),
                  _const_spec((D_MODEL, D_MODEL)), _const_spec((1, D_MODEL)),
                  _const_spec((D_MODEL, D_FF)), _const_spec((D_FF, D_MODEL)), _const_spec((1, D_MODEL))],
        out_specs=row(D_MODEL),
        out_shape=jax.ShapeDtypeStruct((n, D_MODEL), F32),
        compiler_params=pltpu.CompilerParams(dimension_semantics=("parallel",),
                                             vmem_limit_bytes=VMEM_LIMIT),
        name="merge_ffn",
    )(x, fa, fb, fc, w["g1"], w["w_gate"], w["w_a_out"], w["w_b_out"], w["w_c_out"], w["w_o"],
      w["g2"], w["w_up"], w["w_down"], final_g)


def _layer_weights(l, p, lbs):
    w_in = p["w_in"][l]
    cols = lambda a, b: w_in[:, a:b]
    w_a = jnp.concatenate([cols(O_AV, O_QB), cols(O_QB, O_FB), cols(O_QC, O_GA),
                           jnp.pad(cols(O_FB, O_QC), ((0, 0), (0, LANES - HB)))], axis=1).astype(BF16)
    lb = lbs[l]
    row = lambda v: v.reshape(1, -1).astype(F32)
    lbrows = jnp.concatenate([jnp.log(lb)[None], jnp.log1p(-lb)[None], (1.0 - lb)[None],
                              jnp.zeros((5, HC * DK), F32)], axis=0)
    return dict(
        w_a=w_a, g1=row(p["norm1_g"][l]),
        fbias=jnp.pad(p["fox_bf"][l], (0, LANES - HB)).reshape(1, LANES).astype(F32),
        lbrows=lbrows,
        conv_w=jnp.pad(p["conv_w"][l], ((0, HIST_ROWS - CONV_W), (0, 0))),
        conv_b=row(p["conv_b"][l]), ln_g=row(p["conv_ln_g"][l]), ln_b=row(p["conv_ln_b"][l]),
        hn_g=row(p["hgrn_norm_g"][l]),
        w_gate=cols(O_GA, O_END).astype(BF16),
        w_a_out=p["w_a_out"][l].astype(BF16), w_b_out=p["w_b_out"][l].astype(BF16),
        w_c_out=p["w_c_out"][l].astype(BF16), w_o=p["w_o"][l].astype(BF16),
        g2=row(p["norm2_g"][l]), w_up=p["w_up"][l].astype(BF16), w_down=p["w_down"][l].astype(BF16),
    )


def _pad_hist(h):
    return jnp.pad(h, ((0, 0), (HIST_PAD, 0), (0, 0)))


def _trunk_layer(x, w, final_g, final, conv_hist, s0, cache, *, tm_a, tm_d, tl, tq, hg_tile):
    bsz, L, _ = x.shape
    n = bsz * L
    xf = x.reshape(n, D_MODEL)
    u, qs, k, v, lf, qh, lfc, kk, vv, og = _in_proj(xf, w["g1"], w["w_a"], w["fbias"], w["lbrows"], tm_a)
    seq = lambda a: a.reshape(bsz, L, a.shape[-1])

    feat, conv_new = _conv(seq(u), _pad_hist(conv_hist), w["conv_w"], w["conv_b"], w["ln_g"], w["ln_b"], tl)

    lf_t = seq(lf).transpose(0, 2, 1)
    if cache is None:
        hi, mid, lo = _cumsum_bias(lf_t)
        heads = lambda a: seq(a).reshape(bsz, L, HB, DHB).transpose(0, 2, 1, 3)
        col = lambda a: a[..., None]
        zpad = jnp.zeros((bsz, HB, L, LANES - DHB - 3), BF16)
        k_aug = jnp.concatenate([heads(k.astype(BF16)), col(hi), col(mid), col(lo), zpad], axis=-1)
        q_aug = jnp.concatenate([heads(qs), jnp.ones((bsz, HB, L, 3), BF16), zpad], axis=-1)
        vt_blk = seq(v.astype(BF16)).reshape(bsz, L // tq, tq, HB, DHB).transpose(0, 3, 1, 4, 2)
        o_t = _attn_prompt(q_aug, k_aug, vt_blk, tq)
        o_b = o_t.transpose(0, 2, 1).reshape(n, DB)
    else:
        k_cache, v_cache, lf_cache = cache
        P = k_cache.shape[1]
        lf_all = jnp.concatenate([lf_cache.transpose(0, 2, 1), lf_t,
                                  jnp.zeros((bsz, HB, LANES - L), F32)], axis=-1)
        hi, mid, lo = _cumsum_bias(lf_all, pivot=P - 1)
        o_b = _attn_cached(seq(qs), k_cache.reshape(bsz, P, DB), v_cache.reshape(bsz, P, DB),
                           seq(k), seq(v), hi, mid, lo).reshape(n, DB)

    C = min(HGRN_CHUNK, L)
    hg, s_new = _hgrn(seq(qh), seq(lfc), seq(kk), seq(vv), seq(og), s0, w["hn_g"], C, hg_tile)

    x_new = _merge_ffn(xf, feat.reshape(n, CA), o_b, hg.reshape(n, DC), w, final_g, tm_d, final)
    return (x_new.reshape(bsz, L, D_MODEL), seq(k).reshape(bsz, L, HB, DHB), seq(v).reshape(bsz, L, HB, DHB),
            seq(lf), conv_new[:, HIST_PAD:], s_new)


def _lower_bounds(p):
    s = jax.nn.softmax(p.astype(F32), axis=0)
    return jnp.maximum(jnp.cumsum(s, axis=0) - s[0], 0.0)


def kernel(x_prompt, x_sample, cache_fox_k, cache_fox_v, cache_fox_logf, state_conv, state_hgrn,
           norm1_g, w_in, conv_w, conv_b, conv_ln_g, conv_ln_b, w_a_out, fox_bf, w_b_out,
           hgrn_lb_param, hgrn_norm_g, w_c_out, w_o, norm2_g, w_up, w_down, final_g):
    p = dict(norm1_g=norm1_g, w_in=w_in, conv_w=conv_w, conv_b=conv_b, conv_ln_g=conv_ln_g,
             conv_ln_b=conv_ln_b, w_a_out=w_a_out, fox_bf=fox_bf, w_b_out=w_b_out,
             hgrn_norm_g=hgrn_norm_g, w_c_out=w_c_out, w_o=w_o, norm2_g=norm2_g, w_up=w_up, w_down=w_down)
    depth = w_in.shape[0]
    lbs = _lower_bounds(hgrn_lb_param)
    fg = final_g.reshape(1, D_MODEL).astype(F32)
    xp, xs = x_prompt, x_sample
    bp, sp, _ = xp.shape
    bs, ls, _ = xs.shape
    zero_hist = jnp.zeros((bp, CONV_W - 1, CA), F32)
    zero_s = jnp.zeros((bp, HC, DK, DV), F32)
    outs_p, outs_s = [], []
    for l in range(depth):
        w = _layer_weights(l, p, lbs)
        final = l == depth - 1
        rp = _trunk_layer(xp, w, fg, final, zero_hist, zero_s, None,
                          tm_a=min(512, bp * sp), tm_d=min(256, bp * sp), tl=min(512, sp),
                          tq=min(256, sp), hg_tile=min(256, sp))
        rs = _trunk_layer(xs, w, fg, final, state_conv[l], state_hgrn[l],
                          (cache_fox_k[l], cache_fox_v[l], cache_fox_logf[l]),
                          tm_a=min(512, bs * ls), tm_d=min(256, bs * ls), tl=ls, tq=None, hg_tile=ls)
        xp, xs = rp[0], rs[0]
        outs_p.append(rp[1:])
        outs_s.append(rs[1:])
    stack = lambda outs, i: jnp.stack([o[i] for o in outs])
    return (xp, xs,
            stack(outs_p, 0), stack(outs_p, 1), stack(outs_p, 2), stack(outs_p, 3), stack(outs_p, 4),
            stack(outs_s, 0), stack(outs_s, 1), stack(outs_s, 2), stack(outs_s, 3), stack(outs_s, 4))
```

```python
import functools
import math

import numpy as np
import jax
import jax.numpy as jnp
from jax import lax
from jax.experimental import pallas as pl
from jax.experimental.pallas import tpu as pltpu

D_MODEL = 1024
CONV_W = 31
CA = 512
HB = 8
DHB = 64
DB = HB * DHB
HC = 4
DK = 128
DV = 128
DC = HC * DV
D_FF = 4 * D_MODEL
EPS = 1e-6
HGRN_CHUNK = 64

LANES = 128
SUBLANES = 8
VMEM_LIMIT = 56 * 1024 * 1024
NEG_BIG = -1e30
LOG2E = 1.4426950408889634

F32 = jnp.float32
BF16 = jnp.bfloat16

_OFF = np.cumsum([0, CA, CA, DB, DB, DB, HB, HC * DK, HC * DK, DC, DC, D_MODEL, D_MODEL, D_MODEL])
(O_AV, O_AG, O_QB, O_KB, O_VB, O_FB, O_QC, O_FC, O_IC, O_OC, O_GA, O_GB, O_GC, O_END) = [int(v) for v in _OFF]


def _dot(a, b):
    return jnp.dot(a, b, preferred_element_type=F32)


def _dot_nt(a, b):
    return lax.dot_general(a, b, (((1,), (1,)), ((), ())), preferred_element_type=F32)


def _dot_tn(a, b):
    return lax.dot_general(a, b, (((0,), (0,)), ((), ())), preferred_element_type=F32)


def _sigmoid(x):
    return 1.0 / (1.0 + jnp.exp(-x))


def _silu(x):
    return x * _sigmoid(x)


def _rms(x, g):
    return x * lax.rsqrt(jnp.mean(x * x, axis=-1, keepdims=True) + EPS) * g


def _split3(x):
    hi = x.astype(BF16)
    r = x - hi.astype(F32)
    mid = r.astype(BF16)
    lo = (r - mid.astype(F32)).astype(BF16)
    return hi, mid, lo


def _const_spec(shape):
    nd = len(shape)
    return pl.BlockSpec(shape, lambda *_: (0,) * nd, pipeline_mode=pl.Buffered(1))


A_WIDTH = 9 * 512 + LANES


def _in_proj_kernel(x_ref, g1_ref, w_ref, fbias_ref, lb_ref, tri_ref,
                    u_ref, qs_ref, k_ref, kb_ref, v_ref, vt_ref, lf_ref, bias_ref,
                    qh_ref, lfc_ref, kk_ref, vv_ref, og_ref, carry_ref, *, tiles_per_seq):
    i = pl.program_id(0)
    x = x_ref[...]
    h = _rms(x, g1_ref[...]).astype(BF16)

    def seg(i, width=512):
        return _dot(h, w_ref[:, i * 512:i * 512 + width])

    u_ref[...] = seg(0) * _sigmoid(seg(1))
    qs_ref[...] = (seg(2) * (LOG2E * DHB ** -0.5)).astype(BF16)
    k = seg(3)
    k_ref[...] = k
    kb_ref[...] = k.astype(BF16)
    v = seg(4)
    v_ref[...] = v
    vt_ref[...] = v.T.astype(BF16)
    qh_ref[...] = _silu(seg(5)).astype(BF16)

    z = seg(6)
    log_lb = lb_ref[0:1, :]
    log1m_lb = lb_ref[1:2, :]
    one_m_lb = lb_ref[2:3, :]
    e = jnp.exp(-jnp.abs(z))
    ls = jnp.minimum(z, 0.0) - jnp.log(1.0 + e)
    b = log1m_lb + ls
    mx = jnp.maximum(log_lb, b)
    lfc_ref[...] = mx + jnp.log(1.0 + jnp.exp(-jnp.abs(log_lb - b)))
    r = 1.0 / (1.0 + e)
    kk_ref[...] = (one_m_lb * jnp.where(z >= 0.0, e * r, r)).astype(BF16)

    vv_ref[...] = seg(7).astype(BF16)
    og_ref[...] = _silu(seg(8)).astype(BF16)

    f = seg(9, LANES) + fbias_ref[...]
    lf = jnp.minimum(f, 0.0) - jnp.log(1.0 + jnp.exp(-jnp.abs(f)))
    lf_ref[...] = lf[:, :HB]

    @pl.when(i % tiles_per_seq == 0)
    def _():
        carry_ref[...] = jnp.zeros_like(carry_ref)

    hi, mid, lo = _split3(lf)
    tri = tri_ref[...]
    c = (_dot(tri, hi) + _dot(tri, mid)) + _dot(tri, lo) + carry_ref[0:1, :]
    carry_ref[0:1, :] = c[c.shape[0] - 1:, :]
    kbias = -LOG2E * c
    for g in range(HB // HEAD_GROUP):
        bias_ref[g] = kbias[:, g * HEAD_GROUP:(g + 1) * HEAD_GROUP]


def _in_proj(x, g1, w, fbias, lbrows, tm, tiles_per_seq):
    n = x.shape[0]
    tok = lambda w_, dt: jax.ShapeDtypeStruct((n, w_), dt)
    row = lambda w_: pl.BlockSpec((tm, w_), lambda i: (i, 0))
    ng = HB // HEAD_GROUP
    tri = jnp.asarray(np.tril(np.ones((tm, tm), np.float32)), BF16)
    out_shape = (tok(512, F32), tok(512, BF16), tok(512, F32), tok(512, BF16), tok(512, F32),
                 jax.ShapeDtypeStruct((n // tm, DB, tm), BF16), tok(HB, F32),
                 jax.ShapeDtypeStruct((ng, n, HEAD_GROUP), F32),
                 tok(512, BF16), tok(512, F32), tok(512, BF16), tok(512, BF16), tok(512, BF16))
    out_specs = (row(512), row(512), row(512), row(512), row(512),
                 pl.BlockSpec((None, DB, tm), lambda i: (i, 0, 0)), row(HB),
                 pl.BlockSpec((ng, tm, HEAD_GROUP), lambda i: (0, i, 0)),
                 row(512), row(512), row(512), row(512), row(512))
    return pl.pallas_call(
        functools.partial(_in_proj_kernel, tiles_per_seq=tiles_per_seq),
        grid=(n // tm,),
        in_specs=[row(D_MODEL), _const_spec((1, D_MODEL)), _const_spec((D_MODEL, A_WIDTH)),
                  _const_spec((1, LANES)), _const_spec((8, 512)), _const_spec((tm, tm))],
        out_specs=out_specs,
        out_shape=out_shape,
        scratch_shapes=[pltpu.VMEM((8, LANES), F32)],
        compiler_params=pltpu.CompilerParams(dimension_semantics=("arbitrary",),
                                             vmem_limit_bytes=VMEM_LIMIT),
        name="in_proj",
    )(x, g1, w, fbias, lbrows, tri)


def _cumsum_kernel(lf_ref, hi_ref, mid_ref, lo_ref, *, n, pivot):
    x = lf_ref[0]
    lane = lax.broadcasted_iota(jnp.int32, x.shape, 1)
    s = 1
    while s < n:
        x = x + jnp.where(lane >= s, pltpu.roll(x, s, 1), 0.0)
        s *= 2
    piv = jnp.sum(jnp.where(lane == pivot, x, 0.0), axis=1, keepdims=True)
    hi, mid, lo = _split3(LOG2E * (piv - x))
    hi_ref[0] = hi
    mid_ref[0] = mid
    lo_ref[0] = lo


def _cumsum_bias(lf_t, pivot):
    bsz, hh, n = lf_t.shape
    spec = pl.BlockSpec((1, hh, n), lambda b: (b, 0, 0))
    sds = jax.ShapeDtypeStruct((bsz, hh, n), BF16)
    return pl.pallas_call(
        functools.partial(_cumsum_kernel, n=n, pivot=pivot),
        grid=(bsz,),
        in_specs=[spec],
        out_specs=(spec, spec, spec),
        out_shape=(sds, sds, sds),
        compiler_params=pltpu.CompilerParams(dimension_semantics=("parallel",)),
        name="cumsum_bias",
    )(lf_t)


HIST_ROWS = 32
HIST_PAD = HIST_ROWS - (CONV_W - 1)


def _conv_kernel(u_ref, hist_ref, w_ref, cb_ref, g_ref, b_ref, feat_ref, new_ref, buf_ref, *, tl, rc):
    t = pl.program_id(1)

    @pl.when(t == 0)
    def _():
        buf_ref[0:HIST_ROWS, :] = hist_ref[0]

    @pl.when(t > 0)
    def _():
        buf_ref[0:HIST_ROWS, :] = buf_ref[tl:tl + HIST_ROWS, :]

    buf_ref[HIST_ROWS:HIST_ROWS + tl, :] = u_ref[0]
    new_ref[0] = buf_ref[tl:tl + HIST_ROWS, :]

    cb = cb_ref[...]
    g = g_ref[...]
    b = b_ref[...]
    for r0 in range(0, tl, rc):
        y = cb
        for r in range(SUBLANES):
            z = None
            for a in range((HIST_ROWS + SUBLANES) // SUBLANES):
                j = SUBLANES * a + r - HIST_PAD
                if 0 <= j < CONV_W:
                    nrows = rc + (SUBLANES if r else 0)
                    term = w_ref[j:j + 1, :] * buf_ref[r0 + SUBLANES * a:r0 + SUBLANES * a + nrows, :]
                    z = term if z is None else z + term
            y = y + z[r:r + rc]
        mu = jnp.mean(y, axis=-1, keepdims=True)
        yc = y - mu
        yn = yc * lax.rsqrt(jnp.mean(yc * yc, axis=-1, keepdims=True) + EPS) * g + b
        feat_ref[0, r0:r0 + rc, :] = _silu(yn).astype(BF16)


def _conv(u, hist, w, cb, g, b, tl):
    bsz, L, _ = u.shape
    rc = min(tl, 64)
    vec = _const_spec((1, CA))
    return pl.pallas_call(
        functools.partial(_conv_kernel, tl=tl, rc=rc),
        grid=(bsz, L // tl),
        in_specs=[pl.BlockSpec((1, tl, CA), lambda i, t: (i, t, 0)),
                  pl.BlockSpec((1, HIST_ROWS, CA), lambda i, t: (i, 0, 0)),
                  _const_spec((HIST_ROWS, CA)), vec, vec, vec],
        out_specs=(pl.BlockSpec((1, tl, CA), lambda i, t: (i, t, 0)),
                   pl.BlockSpec((1, HIST_ROWS, CA), lambda i, t: (i, 0, 0))),
        out_shape=(jax.ShapeDtypeStruct((bsz, L, CA), BF16),
                   jax.ShapeDtypeStruct((bsz, HIST_ROWS, CA), F32)),
        scratch_shapes=[pltpu.VMEM((tl + HIST_ROWS, CA), F32)],
        compiler_params=pltpu.CompilerParams(dimension_semantics=("parallel", "arbitrary")),
        name="conv",
    )(u, hist, w, cb, g, b)


ONES_ROWS = 16
HEAD_GROUP = 4
GROUP_LANES = HEAD_GROUP * DHB


def _attn_kernel(q_ref, k_ref, vt_ref, bias_ref, o_ref, *, tq):
    i = pl.program_id(2)
    q = q_ref[...]
    lane_head = lax.broadcasted_iota(jnp.int32, q.shape, 1) // DHB
    qm = [jnp.where(lane_head == h, q, jnp.zeros_like(q)) for h in range(HEAD_GROUP)]
    ones = jnp.ones((ONES_ROWS, tq), BF16)

    def step(j, carry, diag):
        rows = pl.ds(pl.multiple_of(j * tq, tq), tq)
        kblk = k_ref[rows, :]
        if diag:
            kk = lax.broadcasted_iota(jnp.int32, (tq, tq), 0)
            qq = lax.broadcasted_iota(jnp.int32, (tq, tq), 1)
            keep = kk <= qq
        ss = [_dot_nt(kblk, qm[h]) + bias_ref[rows, h:h + 1] for h in range(HEAD_GROUP)]
        if diag:
            ss = [jnp.where(keep, s, NEG_BIG) for s in ss]
        ms = [jnp.maximum(carry[h][0], jnp.max(ss[h], axis=0, keepdims=True)) for h in range(HEAD_GROUP)]
        ps = [jnp.exp2(ss[h] - ms[h]).astype(BF16) for h in range(HEAD_GROUP)]
        out = []
        for h in range(HEAD_GROUP):
            m, acc = carry[h]
            alpha = jnp.exp2(m - ms[h])
            vt = jnp.concatenate([vt_ref[j, h * DHB:(h + 1) * DHB, :], ones], axis=0)
            out.append((ms[h], acc * alpha + _dot(vt, ps[h])))
        return tuple(out)

    init = tuple((jnp.full((1, tq), NEG_BIG, F32), jnp.zeros((DHB + ONES_ROWS, tq), F32))
                 for _ in range(HEAD_GROUP))
    carry = lax.fori_loop(0, i, lambda j, c: step(j, c, False), init)
    carry = step(i, carry, True)
    o_t = jnp.concatenate([acc[:DHB] / acc[DHB:DHB + 1] for _, acc in carry], axis=0)
    o_ref[...] = o_t.T.astype(BF16)


def _attn_prompt(qs, kb, vt, bias, bsz, S, tq):
    nq = S // tq
    ng = HB // HEAD_GROUP
    return pl.pallas_call(
        functools.partial(_attn_kernel, tq=tq),
        grid=(bsz, ng, nq),
        in_specs=[pl.BlockSpec((tq, GROUP_LANES), lambda b, g, i: (b * nq + i, g)),
                  pl.BlockSpec((S, GROUP_LANES), lambda b, g, i: (b, g)),
                  pl.BlockSpec((nq, GROUP_LANES, tq), lambda b, g, i: (b, g, 0)),
                  pl.BlockSpec((None, S, HEAD_GROUP), lambda b, g, i: (g, b, 0))],
        out_specs=pl.BlockSpec((tq, GROUP_LANES), lambda b, g, i: (b * nq + i, g)),
        out_shape=jax.ShapeDtypeStruct((bsz * S, DB), BF16),
        compiler_params=pltpu.CompilerParams(dimension_semantics=("parallel", "parallel", "arbitrary"),
                                             vmem_limit_bytes=VMEM_LIMIT),
        name="attn_prompt",
    )(qs, kb, vt, bias)


def _attn_cached_kernel(q_ref, kc_ref, vc_ref, kn_ref, vn_ref, bhi_ref, bmid_ref, blo_ref, o_ref, *, L, P):
    q = q_ref[0]
    lane_head = lax.broadcasted_iota(jnp.int32, (L, DB), 1) // DHB
    q_bd = jnp.concatenate([jnp.where(lane_head == h, q, jnp.zeros_like(q)) for h in range(HB)], axis=0)
    kc = kc_ref[0].astype(BF16)
    vc = vc_ref[0].astype(BF16)
    zpad = jnp.zeros((LANES - L, DB), BF16)
    kn = jnp.concatenate([kn_ref[0].astype(BF16), zpad], axis=0)
    vn = jnp.concatenate([vn_ref[0].astype(BF16), zpad], axis=0)
    bias = (bhi_ref[0].astype(F32) + bmid_ref[0].astype(F32)) + blo_ref[0].astype(F32)

    s_c = _dot_nt(q_bd, kc)
    s_n = _dot_nt(q_bd, kn)
    tt = lax.broadcasted_iota(jnp.int32, (L, LANES), 0)
    uu = lax.broadcasted_iota(jnp.int32, (L, LANES), 1)
    causal = uu <= tt
    pcs, pns, ls = [], [], []
    for h in range(HB):
        sc = s_c[h * L:(h + 1) * L, :] + bias[h:h + 1, :P]
        sn = jnp.where(causal, s_n[h * L:(h + 1) * L, :] + bias[h:h + 1, P:], NEG_BIG)
        m = jnp.maximum(jnp.max(sc, axis=1, keepdims=True), jnp.max(sn, axis=1, keepdims=True))
        pc = jnp.exp2(sc - m)
        pn = jnp.exp2(sn - m)
        ls.append(jnp.sum(pc, axis=1, keepdims=True) + jnp.sum(pn, axis=1, keepdims=True))
        pcs.append(pc.astype(BF16))
        pns.append(pn.astype(BF16))
    o_all = _dot(jnp.concatenate(pcs, axis=0), vc) + _dot(jnp.concatenate(pns, axis=0), vn)
    out = jnp.zeros((L, DB), F32)
    for h in range(HB):
        out = out + jnp.where(lane_head == h, o_all[h * L:(h + 1) * L, :] / ls[h], 0.0)
    o_ref[0] = out.astype(BF16)


def _attn_cached(qs, k_cache, v_cache, k_new, v_new, bhi, bmid, blo):
    bsz, L, _ = qs.shape
    P = k_cache.shape[1]
    seq = lambda r, w: pl.BlockSpec((1, r, w), lambda b: (b, 0, 0))
    return pl.pallas_call(
        functools.partial(_attn_cached_kernel, L=L, P=P),
        grid=(bsz,),
        in_specs=[seq(L, DB), seq(P, DB), seq(P, DB), seq(L, DB), seq(L, DB),
                  seq(HB, P + LANES), seq(HB, P + LANES), seq(HB, P + LANES)],
        out_specs=seq(L, DB),
        out_shape=jax.ShapeDtypeStruct((bsz, L, DB), BF16),
        compiler_params=pltpu.CompilerParams(dimension_semantics=("parallel",),
                                             vmem_limit_bytes=VMEM_LIMIT),
        name="attn_cached",
    )(qs, k_cache, v_cache, k_new, v_new, bhi, bmid, blo)


def _hgrn_exponent_matrix(C):
    nl = int(math.log2(C))
    rows = []
    idx = np.arange(C)
    for lv in range(nl):
        sz = 1 << lv
        m = np.zeros((C, C), np.float32)
        for t in range(C):
            bnd = ((t >> (lv + 1)) << (lv + 1)) + sz - 1
            if (t >> lv) & 1:
                m[t, (idx > bnd) & (idx <= t)] = 1.0
            else:
                m[t, (idx > t) & (idx <= bnd)] = 1.0
        rows.append(m)
    rows.append((idx[None, :] <= idx[:, None]).astype(np.float32))
    rows.append((idx[None, :] > idx[:, None]).astype(np.float32))
    p = np.concatenate(rows, axis=0)
    return np.concatenate([p, p, p], axis=1)


def _hgrn_kernel(p_ref, qh_ref, lf_ref, kk_ref, vv_ref, og_ref, s0_ref, g_ref, hg_ref, sout_ref, st_ref,
                 *, C, T):
    t = pl.program_id(1)
    nl = int(math.log2(C))

    @pl.when(t == 0)
    def _():
        for h in range(HC):
            st_ref[h] = s0_ref[0, h].T

    row = lax.broadcasted_iota(jnp.int32, (C, C), 0)
    col = lax.broadcasted_iota(jnp.int32, (C, C), 1)
    diff = row ^ col
    masks = [(jnp.right_shift(diff, lv) == 1) & ((jnp.right_shift(row, lv) & 1) == 1) for lv in range(nl)]
    diag = row == col
    g = g_ref[...]
    pmat = p_ref[...]

    for c0 in range(0, T, C):
        hi, mid, lo = _split3(lf_ref[0, c0:c0 + C, :])
        e_all = jnp.exp(_dot(pmat, jnp.concatenate([hi, mid, lo], axis=0)))
        for h in range(HC):
            hs = slice(h * DK, (h + 1) * DK)
            q = qh_ref[0, c0:c0 + C, hs].astype(F32)
            k = kk_ref[0, c0:c0 + C, hs].astype(F32)
            v = vv_ref[0, c0:c0 + C, hs]
            a = jnp.where(diag, _dot_nt(q.astype(BF16), k.astype(BF16)), 0.0)
            for lv in range(nl):
                e = e_all[lv * C:(lv + 1) * C, hs]
                a = a + jnp.where(masks[lv], _dot_nt((q * e).astype(BF16), (k * e).astype(BF16)), 0.0)
            e_q = e_all[nl * C:(nl + 1) * C, hs]
            e_k = e_all[(nl + 1) * C:(nl + 2) * C, hs]
            st = st_ref[h]
            o = _dot(a.astype(BF16), v) + _dot_nt((q * e_q).astype(BF16), st.astype(BF16))
            st_ref[h] = st * e_q[C - 1:C, :] + _dot_tn(v, (k * e_k).astype(BF16))
            o = o * lax.rsqrt(jnp.mean(o * o, axis=-1, keepdims=True) + EPS) * g
            hg_ref[0, c0:c0 + C, hs] = (o * og_ref[0, c0:c0 + C, hs].astype(F32)).astype(BF16)

    @pl.when(t == pl.num_programs(1) - 1)
    def _():
        for h in range(HC):
            sout_ref[0, h] = st_ref[h].T


def _hgrn(qh, lfc, kk, vv, og, s0, g, C, T):
    bsz, L, _ = qh.shape
    pmat = jnp.asarray(_hgrn_exponent_matrix(C), BF16)
    tile = pl.BlockSpec((1, T, DC), lambda b, t: (b, t, 0))
    state = pl.BlockSpec((1, HC, DK, DV), lambda b, t: (b, 0, 0, 0))
    return pl.pallas_call(
        functools.partial(_hgrn_kernel, C=C, T=T),
        grid=(bsz, L // T),
        in_specs=[_const_spec(pmat.shape), tile, tile, tile, tile, tile, state, _const_spec((1, DV))],
        out_specs=(tile, state),
        out_shape=(jax.ShapeDtypeStruct((bsz, L, DC), BF16),
                   jax.ShapeDtypeStruct((bsz, HC, DK, DV), F32)),
        scratch_shapes=[pltpu.VMEM((HC, DV, DK), F32)],
        compiler_params=pltpu.CompilerParams(dimension_semantics=("parallel", "arbitrary")),
        name="hgrn",
    )(pmat, qh, lfc, kk, vv, og, s0, g)


FF_CHUNK = 1024


def _merge_ffn_kernel(x_ref, fa_ref, fb_ref, fc_ref, g1_ref, wg_ref, wa_ref, wb_ref, wc_ref, wo_ref,
                      g2_ref, wup_ref, wdn_ref, gf_ref, o_ref, *, final):
    x = x_ref[...]
    h1 = _rms(x, g1_ref[...]).astype(BF16)
    m = None
    for i, (f_ref, w_ref) in enumerate(((fa_ref, wa_ref), (fb_ref, wb_ref), (fc_ref, wc_ref))):
        gate = _sigmoid(_dot(h1, wg_ref[:, i * D_MODEL:(i + 1) * D_MODEL]))
        y = gate * _dot(f_ref[...], w_ref[...])
        m = y if m is None else m + y
    x = x + _dot(m.astype(BF16), wo_ref[...])
    h2 = _rms(x, g2_ref[...]).astype(BF16)
    acc = None
    for c in range(0, D_FF, FF_CHUNK):
        up = jnp.maximum(_dot(h2, wup_ref[:, c:c + FF_CHUNK]), 0.0)
        d = _dot((up * up).astype(BF16), wdn_ref[c:c + FF_CHUNK, :])
        acc = d if acc is None else acc + d
    x = x + acc
    if final:
        x = _rms(x, gf_ref[...])
    o_ref[...] = x


def _merge_ffn(x, fa, fb, fc, w, final_g, tm, final):
    n = x.shape[0]
    row = lambda w_: pl.BlockSpec((tm, w_), lambda i: (i, 0))
    return pl.pallas_call(
        functools.partial(_merge_ffn_kernel, final=final),
        grid=(n // tm,),
        in_specs=[row(D_MODEL), row(CA), row(DB), row(DC),
                  _const_spec((1, D_MODEL)), _const_spec((D_MODEL, 3 * D_MODEL)),
                  _const_spec((CA, D_MODEL)), _const_spec((DB, D_MODEL)), _const_spec((DC, D_MODEL)),
                  _const_spec((D_MODEL, D_MODEL)), _const_spec((1, D_MODEL)),
                  _const_spec((D_MODEL, D_FF)), _const_spec((D_FF, D_MODEL)), _const_spec((1, D_MODEL))],
        out_specs=row(D_MODEL),
        out_shape=jax.ShapeDtypeStruct((n, D_MODEL), F32),
        compiler_params=pltpu.CompilerParams(dimension_semantics=("parallel",),
                                             vmem_limit_bytes=VMEM_LIMIT),
        name="merge_ffn",
    )(x, fa, fb, fc, w["g1"], w["w_gate"], w["w_a_out"], w["w_b_out"], w["w_c_out"], w["w_o"],
      w["g2"], w["w_up"], w["w_down"], final_g)


def _layer_weights(l, p, lbs):
    w_in = p["w_in"][l]
    cols = lambda a, b: w_in[:, a:b]
    w_a = jnp.concatenate([cols(O_AV, O_QB), cols(O_QB, O_FB), cols(O_QC, O_GA),
                           jnp.pad(cols(O_FB, O_QC), ((0, 0), (0, LANES - HB)))], axis=1).astype(BF16)
    lb = lbs[l]
    row = lambda v: v.reshape(1, -1).astype(F32)
    lbrows = jnp.concatenate([jnp.log(lb)[None], jnp.log1p(-lb)[None], (1.0 - lb)[None],
                              jnp.zeros((5, HC * DK), F32)], axis=0)
    return dict(
        w_a=w_a, g1=row(p["norm1_g"][l]),
        fbias=jnp.pad(p["fox_bf"][l], (0, LANES - HB)).reshape(1, LANES).astype(F32),
        lbrows=lbrows,
        conv_w=jnp.pad(p["conv_w"][l], ((0, HIST_ROWS - CONV_W), (0, 0))),
        conv_b=row(p["conv_b"][l]), ln_g=row(p["conv_ln_g"][l]), ln_b=row(p["conv_ln_b"][l]),
        hn_g=row(p["hgrn_norm_g"][l]),
        w_gate=cols(O_GA, O_END).astype(BF16),
        w_a_out=p["w_a_out"][l].astype(BF16), w_b_out=p["w_b_out"][l].astype(BF16),
        w_c_out=p["w_c_out"][l].astype(BF16), w_o=p["w_o"][l].astype(BF16),
        g2=row(p["norm2_g"][l]), w_up=p["w_up"][l].astype(BF16), w_down=p["w_down"][l].astype(BF16),
    )


def _pad_hist(h):
    return jnp.pad(h, ((0, 0), (HIST_PAD, 0), (0, 0)))


def _trunk_layer(x, w, final_g, final, conv_hist, s0, cache, *, tm_a, tm_d, tl, tq, hg_tile):
    bsz, L, _ = x.shape
    n = bsz * L
    xf = x.reshape(n, D_MODEL)
    u, qs, k, kb, v, vt, lf, bias, qh, lfc, kk, vv, og = _in_proj(
        xf, w["g1"], w["w_a"], w["fbias"], w["lbrows"], tm_a, max(L // tm_a, 1))
    seq = lambda a: a.reshape(bsz, L, a.shape[-1])

    feat, conv_new = _conv(seq(u), _pad_hist(conv_hist), w["conv_w"], w["conv_b"], w["ln_g"], w["ln_b"], tl)

    if cache is None:
        o_b = _attn_prompt(qs, kb, vt, bias, bsz, L, tq)
    else:
        k_cache, v_cache, lf_cache = cache
        P = k_cache.shape[1]
        lf_t = seq(lf).transpose(0, 2, 1)
        lf_all = jnp.concatenate([lf_cache.transpose(0, 2, 1), lf_t,
                                  jnp.zeros((bsz, HB, LANES - L), F32)], axis=-1)
        hi, mid, lo = _cumsum_bias(lf_all, pivot=P - 1)
        o_b = _attn_cached(seq(qs), k_cache.reshape(bsz, P, DB), v_cache.reshape(bsz, P, DB),
                           seq(k), seq(v), hi, mid, lo).reshape(n, DB)

    C = min(HGRN_CHUNK, L)
    hg, s_new = _hgrn(seq(qh), seq(lfc), seq(kk), seq(vv), seq(og), s0, w["hn_g"], C, hg_tile)

    x_new = _merge_ffn(xf, feat.reshape(n, CA), o_b, hg.reshape(n, DC), w, final_g, tm_d, final)
    return (x_new.reshape(bsz, L, D_MODEL), seq(k).reshape(bsz, L, HB, DHB), seq(v).reshape(bsz, L, HB, DHB),
            seq(lf), conv_new[:, HIST_PAD:], s_new)


def _lower_bounds(p):
    s = jax.nn.softmax(p.astype(F32), axis=0)
    return jnp.maximum(jnp.cumsum(s, axis=0) - s[0], 0.0)


def kernel(x_prompt, x_sample, cache_fox_k, cache_fox_v, cache_fox_logf, state_conv, state_hgrn,
           norm1_g, w_in, conv_w, conv_b, conv_ln_g, conv_ln_b, w_a_out, fox_bf, w_b_out,
           hgrn_lb_param, hgrn_norm_g, w_c_out, w_o, norm2_g, w_up, w_down, final_g):
    p = dict(norm1_g=norm1_g, w_in=w_in, conv_w=conv_w, conv_b=conv_b, conv_ln_g=conv_ln_g,
             conv_ln_b=conv_ln_b, w_a_out=w_a_out, fox_bf=fox_bf, w_b_out=w_b_out,
             hgrn_norm_g=hgrn_norm_g, w_c_out=w_c_out, w_o=w_o, norm2_g=norm2_g, w_up=w_up, w_down=w_down)
    depth = w_in.shape[0]
    lbs = _lower_bounds(hgrn_lb_param)
    fg = final_g.reshape(1, D_MODEL).astype(F32)
    xp, xs = x_prompt, x_sample
    bp, sp, _ = xp.shape
    bs, ls, _ = xs.shape
    zero_hist = jnp.zeros((bp, CONV_W - 1, CA), F32)
    zero_s = jnp.zeros((bp, HC, DK, DV), F32)
    outs_p, outs_s = [], []
    for l in range(depth):
        w = _layer_weights(l, p, lbs)
        final = l == depth - 1
        rp = _trunk_layer(xp, w, fg, final, zero_hist, zero_s, None,
                          tm_a=min(512, bp * sp), tm_d=min(256, bp * sp), tl=min(512, sp),
                          tq=min(512, sp), hg_tile=min(256, sp))
        rs = _trunk_layer(xs, w, fg, final, state_conv[l], state_hgrn[l],
                          (cache_fox_k[l], cache_fox_v[l], cache_fox_logf[l]),
                          tm_a=min(512, bs * ls), tm_d=min(256, bs * ls), tl=ls, tq=None, hg_tile=ls)
        xp, xs = rp[0], rs[0]
        outs_p.append(rp[1:])
        outs_s.append(rs[1:])
    stack = lambda outs, i: jnp.stack([o[i] for o in outs])
    return (xp, xs,
            stack(outs_p, 0), stack(outs_p, 1), stack(outs_p, 2), stack(outs_p, 3), stack(outs_p, 4),
            stack(outs_s, 0), stack(outs_s, 1), stack(outs_s, 2), stack(outs_s, 3), stack(outs_s, 4))
```

```python
import functools
import math

import numpy as np
import jax
import jax.numpy as jnp
from jax import lax
from jax.experimental import pallas as pl
from jax.experimental.pallas import tpu as pltpu

D_MODEL = 1024
CONV_W = 31
CA = 512
HB = 8
DHB = 64
DB = HB * DHB
HC = 4
DK = 128
DV = 128
DC = HC * DV
D_FF = 4 * D_MODEL
EPS = 1e-6
HGRN_CHUNK = 128

LANES = 128
SUBLANES = 8
VMEM_LIMIT = 56 * 1024 * 1024
NEG_BIG = -1e30
LOG2E = 1.4426950408889634

F32 = jnp.float32
BF16 = jnp.bfloat16

_OFF = np.cumsum([0, CA, CA, DB, DB, DB, HB, HC * DK, HC * DK, DC, DC, D_MODEL, D_MODEL, D_MODEL])
(O_AV, O_AG, O_QB, O_KB, O_VB, O_FB, O_QC, O_FC, O_IC, O_OC, O_GA, O_GB, O_GC, O_END) = [int(v) for v in _OFF]


def _dot(a, b):
    return jnp.dot(a, b, preferred_element_type=F32)


def _dot_nt(a, b):
    return lax.dot_general(a, b, (((1,), (1,)), ((), ())), preferred_element_type=F32)


def _dot_tn(a, b):
    return lax.dot_general(a, b, (((0,), (0,)), ((), ())), preferred_element_type=F32)


def _sigmoid(x):
    return 1.0 / (1.0 + jnp.exp(-x))


def _silu(x):
    return x * _sigmoid(x)


def _rms(x, g):
    return x * lax.rsqrt(jnp.mean(x * x, axis=-1, keepdims=True) + EPS) * g


def _split3(x):
    hi = x.astype(BF16)
    r = x - hi.astype(F32)
    mid = r.astype(BF16)
    lo = (r - mid.astype(F32)).astype(BF16)
    return hi, mid, lo


def _const_spec(shape):
    nd = len(shape)
    return pl.BlockSpec(shape, lambda *_: (0,) * nd, pipeline_mode=pl.Buffered(1))


A_WIDTH = 9 * 512 + LANES


def _in_proj_kernel(x_ref, g1_ref, w_ref, fbias_ref, lb_ref, tri_ref,
                    u_ref, qs_ref, k_ref, kb_ref, v_ref, vt_ref, lf_ref, bias_ref,
                    qh_ref, lfc_ref, kk_ref, vv_ref, og_ref, carry_ref, *, tiles_per_seq):
    i = pl.program_id(0)
    x = x_ref[...]
    h = _rms(x, g1_ref[...]).astype(BF16)

    def seg(i, width=512):
        return _dot(h, w_ref[:, i * 512:i * 512 + width])

    u_ref[...] = seg(0) * _sigmoid(seg(1))
    qs_ref[...] = (seg(2) * (LOG2E * DHB ** -0.5)).astype(BF16)
    k = seg(3)
    k_ref[...] = k
    kb_ref[...] = k.astype(BF16)
    v = seg(4)
    v_ref[...] = v
    vt_ref[...] = v.T.astype(BF16)
    qh_ref[...] = _silu(seg(5)).astype(BF16)

    z = seg(6)
    log_lb = lb_ref[0:1, :]
    log1m_lb = lb_ref[1:2, :]
    one_m_lb = lb_ref[2:3, :]
    e = jnp.exp(-jnp.abs(z))
    ls = jnp.minimum(z, 0.0) - jnp.log(1.0 + e)
    b = log1m_lb + ls
    mx = jnp.maximum(log_lb, b)
    lfc_ref[...] = mx + jnp.log(1.0 + jnp.exp(-jnp.abs(log_lb - b)))
    r = 1.0 / (1.0 + e)
    kk_ref[...] = (one_m_lb * jnp.where(z >= 0.0, e * r, r)).astype(BF16)

    vv_ref[...] = seg(7).astype(BF16)
    og_ref[...] = _silu(seg(8)).astype(BF16)

    f = seg(9, LANES) + fbias_ref[...]
    lf = jnp.minimum(f, 0.0) - jnp.log(1.0 + jnp.exp(-jnp.abs(f)))
    lf_ref[...] = lf[:, :HB]

    @pl.when(i % tiles_per_seq == 0)
    def _():
        carry_ref[...] = jnp.zeros_like(carry_ref)

    hi, mid, lo = _split3(lf)
    tri = tri_ref[...]
    c = (_dot(tri, hi) + _dot(tri, mid)) + _dot(tri, lo) + carry_ref[0:1, :]
    carry_ref[0:1, :] = c[c.shape[0] - 1:, :]
    kbias = -LOG2E * c
    for g in range(HB // HEAD_GROUP):
        bias_ref[g] = kbias[:, g * HEAD_GROUP:(g + 1) * HEAD_GROUP]


def _in_proj(x, g1, w, fbias, lbrows, tm, tiles_per_seq):
    n = x.shape[0]
    tok = lambda w_, dt: jax.ShapeDtypeStruct((n, w_), dt)
    row = lambda w_: pl.BlockSpec((tm, w_), lambda i: (i, 0))
    ng = HB // HEAD_GROUP
    tri = jnp.asarray(np.tril(np.ones((tm, tm), np.float32)), BF16)
    out_shape = (tok(512, F32), tok(512, BF16), tok(512, F32), tok(512, BF16), tok(512, F32),
                 jax.ShapeDtypeStruct((n // tm, DB, tm), BF16), tok(HB, F32),
                 jax.ShapeDtypeStruct((ng, n, HEAD_GROUP), F32),
                 tok(512, BF16), tok(512, F32), tok(512, BF16), tok(512, BF16), tok(512, BF16))
    out_specs = (row(512), row(512), row(512), row(512), row(512),
                 pl.BlockSpec((None, DB, tm), lambda i: (i, 0, 0)), row(HB),
                 pl.BlockSpec((ng, tm, HEAD_GROUP), lambda i: (0, i, 0)),
                 row(512), row(512), row(512), row(512), row(512))
    return pl.pallas_call(
        functools.partial(_in_proj_kernel, tiles_per_seq=tiles_per_seq),
        grid=(n // tm,),
        in_specs=[row(D_MODEL), _const_spec((1, D_MODEL)), _const_spec((D_MODEL, A_WIDTH)),
                  _const_spec((1, LANES)), _const_spec((8, 512)), _const_spec((tm, tm))],
        out_specs=out_specs,
        out_shape=out_shape,
        scratch_shapes=[pltpu.VMEM((8, LANES), F32)],
        compiler_params=pltpu.CompilerParams(dimension_semantics=("arbitrary",),
                                             vmem_limit_bytes=VMEM_LIMIT),
        name="in_proj",
    )(x, g1, w, fbias, lbrows, tri)


def _cumsum_kernel(lf_ref, hi_ref, mid_ref, lo_ref, *, n, pivot):
    x = lf_ref[0]
    lane = lax.broadcasted_iota(jnp.int32, x.shape, 1)
    s = 1
    while s < n:
        x = x + jnp.where(lane >= s, pltpu.roll(x, s, 1), 0.0)
        s *= 2
    piv = jnp.sum(jnp.where(lane == pivot, x, 0.0), axis=1, keepdims=True)
    hi, mid, lo = _split3(LOG2E * (piv - x))
    hi_ref[0] = hi
    mid_ref[0] = mid
    lo_ref[0] = lo


def _cumsum_bias(lf_t, pivot):
    bsz, hh, n = lf_t.shape
    spec = pl.BlockSpec((1, hh, n), lambda b: (b, 0, 0))
    sds = jax.ShapeDtypeStruct((bsz, hh, n), BF16)
    return pl.pallas_call(
        functools.partial(_cumsum_kernel, n=n, pivot=pivot),
        grid=(bsz,),
        in_specs=[spec],
        out_specs=(spec, spec, spec),
        out_shape=(sds, sds, sds),
        compiler_params=pltpu.CompilerParams(dimension_semantics=("parallel",)),
        name="cumsum_bias",
    )(lf_t)


HIST_ROWS = 32
HIST_PAD = HIST_ROWS - (CONV_W - 1)


def _conv_kernel(u_ref, hist_ref, w_ref, cb_ref, g_ref, b_ref, feat_ref, new_ref, buf_ref, *, tl, rc):
    t = pl.program_id(1)

    @pl.when(t == 0)
    def _():
        buf_ref[0:HIST_ROWS, :] = hist_ref[0]

    @pl.when(t > 0)
    def _():
        buf_ref[0:HIST_ROWS, :] = buf_ref[tl:tl + HIST_ROWS, :]

    buf_ref[HIST_ROWS:HIST_ROWS + tl, :] = u_ref[0]
    new_ref[0] = buf_ref[tl:tl + HIST_ROWS, :]

    cb = cb_ref[...]
    g = g_ref[...]
    b = b_ref[...]
    for r0 in range(0, tl, rc):
        y = cb
        for r in range(SUBLANES):
            z = None
            for a in range((HIST_ROWS + SUBLANES) // SUBLANES):
                j = SUBLANES * a + r - HIST_PAD
                if 0 <= j < CONV_W:
                    nrows = rc + (SUBLANES if r else 0)
                    term = w_ref[j:j + 1, :] * buf_ref[r0 + SUBLANES * a:r0 + SUBLANES * a + nrows, :]
                    z = term if z is None else z + term
            y = y + z[r:r + rc]
        mu = jnp.mean(y, axis=-1, keepdims=True)
        yc = y - mu
        yn = yc * lax.rsqrt(jnp.mean(yc * yc, axis=-1, keepdims=True) + EPS) * g + b
        feat_ref[0, r0:r0 + rc, :] = _silu(yn).astype(BF16)


def _conv(u, hist, w, cb, g, b, tl):
    bsz, L, _ = u.shape
    rc = min(tl, 64)
    vec = _const_spec((1, CA))
    return pl.pallas_call(
        functools.partial(_conv_kernel, tl=tl, rc=rc),
        grid=(bsz, L // tl),
        in_specs=[pl.BlockSpec((1, tl, CA), lambda i, t: (i, t, 0)),
                  pl.BlockSpec((1, HIST_ROWS, CA), lambda i, t: (i, 0, 0)),
                  _const_spec((HIST_ROWS, CA)), vec, vec, vec],
        out_specs=(pl.BlockSpec((1, tl, CA), lambda i, t: (i, t, 0)),
                   pl.BlockSpec((1, HIST_ROWS, CA), lambda i, t: (i, 0, 0))),
        out_shape=(jax.ShapeDtypeStruct((bsz, L, CA), BF16),
                   jax.ShapeDtypeStruct((bsz, HIST_ROWS, CA), F32)),
        scratch_shapes=[pltpu.VMEM((tl + HIST_ROWS, CA), F32)],
        compiler_params=pltpu.CompilerParams(dimension_semantics=("parallel", "arbitrary")),
        name="conv",
    )(u, hist, w, cb, g, b)


ONES_ROWS = 16
HEAD_GROUP = 4
GROUP_LANES = HEAD_GROUP * DHB


def _attn_kernel(q_ref, k_ref, vt_ref, bias_ref, o_ref, *, tq):
    i = pl.program_id(2)
    q = q_ref[...]
    lane_head = lax.broadcasted_iota(jnp.int32, q.shape, 1) // DHB
    qm = [jnp.where(lane_head == h, q, jnp.zeros_like(q)) for h in range(HEAD_GROUP)]
    ones = jnp.ones((ONES_ROWS, tq), BF16)

    def step(j, carry, diag):
        rows = pl.ds(pl.multiple_of(j * tq, tq), tq)
        kblk = k_ref[rows, :]
        if diag:
            kk = lax.broadcasted_iota(jnp.int32, (tq, tq), 0)
            qq = lax.broadcasted_iota(jnp.int32, (tq, tq), 1)
            keep = kk <= qq
        ss = [_dot_nt(kblk, qm[h]) + bias_ref[rows, h:h + 1] for h in range(HEAD_GROUP)]
        if diag:
            ss = [jnp.where(keep, s, NEG_BIG) for s in ss]
        ms = [jnp.maximum(carry[h][0], jnp.max(ss[h], axis=0, keepdims=True)) for h in range(HEAD_GROUP)]
        ps = [jnp.exp2(ss[h] - ms[h]).astype(BF16) for h in range(HEAD_GROUP)]
        out = []
        for h in range(HEAD_GROUP):
            m, acc = carry[h]
            alpha = jnp.exp2(m - ms[h])
            vt = jnp.concatenate([vt_ref[j, h * DHB:(h + 1) * DHB, :], ones], axis=0)
            out.append((ms[h], acc * alpha + _dot(vt, ps[h])))
        return tuple(out)

    init = tuple((jnp.full((1, tq), NEG_BIG, F32), jnp.zeros((DHB + ONES_ROWS, tq), F32))
                 for _ in range(HEAD_GROUP))
    carry = lax.fori_loop(0, i, lambda j, c: step(j, c, False), init)
    carry = step(i, carry, True)
    o_t = jnp.concatenate([acc[:DHB] / acc[DHB:DHB + 1] for _, acc in carry], axis=0)
    o_ref[...] = o_t.T.astype(BF16)


def _attn_prompt(qs, kb, vt, bias, bsz, S, tq):
    nq = S // tq
    ng = HB // HEAD_GROUP
    return pl.pallas_call(
        functools.partial(_attn_kernel, tq=tq),
        grid=(bsz, ng, nq),
        in_specs=[pl.BlockSpec((tq, GROUP_LANES), lambda b, g, i: (b * nq + i, g)),
                  pl.BlockSpec((S, GROUP_LANES), lambda b, g, i: (b, g)),
                  pl.BlockSpec((nq, GROUP_LANES, tq), lambda b, g, i: (b, g, 0)),
                  pl.BlockSpec((None, S, HEAD_GROUP), lambda b, g, i: (g, b, 0))],
        out_specs=pl.BlockSpec((tq, GROUP_LANES), lambda b, g, i: (b * nq + i, g)),
        out_shape=jax.ShapeDtypeStruct((bsz * S, DB), BF16),
        compiler_params=pltpu.CompilerParams(dimension_semantics=("parallel", "parallel", "arbitrary"),
                                             vmem_limit_bytes=VMEM_LIMIT),
        name="attn_prompt",
    )(qs, kb, vt, bias)


def _attn_cached_kernel(q_ref, kc_ref, vc_ref, kn_ref, vn_ref, bhi_ref, bmid_ref, blo_ref, o_ref, *, L, P):
    q = q_ref[0]
    lane_head = lax.broadcasted_iota(jnp.int32, (L, DB), 1) // DHB
    q_bd = jnp.concatenate([jnp.where(lane_head == h, q, jnp.zeros_like(q)) for h in range(HB)], axis=0)
    kc = kc_ref[...].astype(BF16)
    vc = vc_ref[...].astype(BF16)
    zpad = jnp.zeros((LANES - L, DB), BF16)
    kn = jnp.concatenate([kn_ref[0].astype(BF16), zpad], axis=0)
    vn = jnp.concatenate([vn_ref[0].astype(BF16), zpad], axis=0)
    bias = (bhi_ref[0].astype(F32) + bmid_ref[0].astype(F32)) + blo_ref[0].astype(F32)

    s_c = _dot(q_bd, kc)
    s_n = _dot_nt(q_bd, kn)
    tt = lax.broadcasted_iota(jnp.int32, (L, LANES), 0)
    uu = lax.broadcasted_iota(jnp.int32, (L, LANES), 1)
    causal = uu <= tt
    pcs, pns, ls = [], [], []
    for h in range(HB):
        sc = s_c[h * L:(h + 1) * L, :] + bias[h:h + 1, :P]
        sn = jnp.where(causal, s_n[h * L:(h + 1) * L, :] + bias[h:h + 1, P:], NEG_BIG)
        m = jnp.maximum(jnp.max(sc, axis=1, keepdims=True), jnp.max(sn, axis=1, keepdims=True))
        pc = jnp.exp2(sc - m)
        pn = jnp.exp2(sn - m)
        ls.append(jnp.sum(pc, axis=1, keepdims=True) + jnp.sum(pn, axis=1, keepdims=True))
        pcs.append(pc.astype(BF16))
        pns.append(pn.astype(BF16))
    o_all = _dot_nt(jnp.concatenate(pcs, axis=0), vc) + _dot(jnp.concatenate(pns, axis=0), vn)
    out = jnp.zeros((L, DB), F32)
    for h in range(HB):
        out = out + jnp.where(lane_head == h, o_all[h * L:(h + 1) * L, :] / ls[h], 0.0)
    o_ref[0] = out.astype(BF16)


def _attn_cached(qs, k_cache_t, v_cache_t, layer, k_new, v_new, bhi, bmid, blo):
    bsz, L, _ = qs.shape
    P = k_cache_t.shape[-1]
    seq = lambda r, w: pl.BlockSpec((1, r, w), lambda b: (b, 0, 0))
    cache = pl.BlockSpec((None, None, DB, P), lambda b: (layer, b, 0, 0))
    return pl.pallas_call(
        functools.partial(_attn_cached_kernel, L=L, P=P),
        grid=(bsz,),
        in_specs=[seq(L, DB), cache, cache, seq(L, DB), seq(L, DB),
                  seq(HB, P + LANES), seq(HB, P + LANES), seq(HB, P + LANES)],
        out_specs=seq(L, DB),
        out_shape=jax.ShapeDtypeStruct((bsz, L, DB), BF16),
        compiler_params=pltpu.CompilerParams(dimension_semantics=("parallel",),
                                             vmem_limit_bytes=VMEM_LIMIT),
        name="attn_cached",
    )(qs, k_cache_t, v_cache_t, k_new, v_new, bhi, bmid, blo)


def _hgrn_exponent_matrix(C):
    nl = int(math.log2(C))
    rows = []
    idx = np.arange(C)
    for lv in range(nl):
        sz = 1 << lv
        m = np.zeros((C, C), np.float32)
        for t in range(C):
            bnd = ((t >> (lv + 1)) << (lv + 1)) + sz - 1
            if (t >> lv) & 1:
                m[t, (idx > bnd) & (idx <= t)] = 1.0
            else:
                m[t, (idx > t) & (idx <= bnd)] = 1.0
        rows.append(m)
    rows.append((idx[None, :] <= idx[:, None]).astype(np.float32))
    rows.append((idx[None, :] > idx[:, None]).astype(np.float32))
    p = np.concatenate(rows, axis=0)
    return np.concatenate([p, p], axis=1)


def _hgrn_kernel(p_ref, qh_ref, lf_ref, kk_ref, vv_ref, og_ref, s0_ref, g_ref, hg_ref, sout_ref, st_ref,
                 *, C, T):
    t = pl.program_id(1)
    nl = int(math.log2(C))

    @pl.when(t == 0)
    def _():
        for h in range(HC):
            st_ref[h] = s0_ref[0, h].T

    row = lax.broadcasted_iota(jnp.int32, (C, C), 0)
    col = lax.broadcasted_iota(jnp.int32, (C, C), 1)
    diff = row ^ col
    masks = [(jnp.right_shift(diff, lv) == 1) & ((jnp.right_shift(row, lv) & 1) == 1) for lv in range(nl)]
    diag = row == col
    g = g_ref[...]
    pmat = p_ref[...]

    pre = {}
    for c0 in range(0, T, C):
        lf = lf_ref[0, c0:c0 + C, :]
        hi = lf.astype(BF16)
        lo = (lf - hi.astype(F32)).astype(BF16)
        e_all = jnp.exp(_dot(pmat, jnp.concatenate([hi, lo], axis=0)))
        for h in range(HC):
            hs = slice(h * DK, (h + 1) * DK)
            qb = qh_ref[0, c0:c0 + C, hs]
            kb = kk_ref[0, c0:c0 + C, hs]
            q = qb.astype(F32)
            k = kb.astype(F32)
            a = jnp.where(diag, _dot_nt(qb, kb), 0.0)
            for lv in range(nl):
                e = e_all[lv * C:(lv + 1) * C, hs]
                a = a + jnp.where(masks[lv], _dot_nt((q * e).astype(BF16), (k * e).astype(BF16)), 0.0)
            e_q = e_all[nl * C:(nl + 1) * C, hs]
            e_k = e_all[(nl + 1) * C:(nl + 2) * C, hs]
            pre[c0, h] = (a.astype(BF16), (q * e_q).astype(BF16), (k * e_k).astype(BF16), e_q[C - 1:C, :])

    for h in range(HC):
        hs = slice(h * DK, (h + 1) * DK)
        st = st_ref[h]
        for c0 in range(0, T, C):
            a, q_dec, k_dec, chunk_dec = pre[c0, h]
            v = vv_ref[0, c0:c0 + C, hs]
            o = _dot(a, v) + _dot_nt(q_dec, st.astype(BF16))
            st = st * chunk_dec + _dot_tn(v, k_dec)
            o = o * lax.rsqrt(jnp.mean(o * o, axis=-1, keepdims=True) + EPS) * g
            hg_ref[0, c0:c0 + C, hs] = (o * og_ref[0, c0:c0 + C, hs].astype(F32)).astype(BF16)
        st_ref[h] = st

    @pl.when(t == pl.num_programs(1) - 1)
    def _():
        for h in range(HC):
            sout_ref[0, h] = st_ref[h].T


def _hgrn(qh, lfc, kk, vv, og, s0, g, C, T):
    bsz, L, _ = qh.shape
    pmat = jnp.asarray(_hgrn_exponent_matrix(C), BF16)
    tile = pl.BlockSpec((1, T, DC), lambda b, t: (b, t, 0))
    state = pl.BlockSpec((1, HC, DK, DV), lambda b, t: (b, 0, 0, 0))
    return pl.pallas_call(
        functools.partial(_hgrn_kernel, C=C, T=T),
        grid=(bsz, L // T),
        in_specs=[_const_spec(pmat.shape), tile, tile, tile, tile, tile, state, _const_spec((1, DV))],
        out_specs=(tile, state),
        out_shape=(jax.ShapeDtypeStruct((bsz, L, DC), BF16),
                   jax.ShapeDtypeStruct((bsz, HC, DK, DV), F32)),
        scratch_shapes=[pltpu.VMEM((HC, DV, DK), F32)],
        compiler_params=pltpu.CompilerParams(dimension_semantics=("parallel", "arbitrary")),
        name="hgrn",
    )(pmat, qh, lfc, kk, vv, og, s0, g)


FF_CHUNK = 1024


def _merge_ffn_kernel(x_ref, fa_ref, fb_ref, fc_ref, g1_ref, wg_ref, wa_ref, wb_ref, wc_ref, wo_ref,
                      g2_ref, wup_ref, wdn_ref, gf_ref, o_ref, *, final):
    x = x_ref[...]
    h1 = _rms(x, g1_ref[...]).astype(BF16)
    m = None
    for i, (f_ref, w_ref) in enumerate(((fa_ref, wa_ref), (fb_ref, wb_ref), (fc_ref, wc_ref))):
        gate = _sigmoid(_dot(h1, wg_ref[:, i * D_MODEL:(i + 1) * D_MODEL]))
        y = gate * _dot(f_ref[...], w_ref[...])
        m = y if m is None else m + y
    x = x + _dot(m.astype(BF16), wo_ref[...])
    h2 = _rms(x, g2_ref[...]).astype(BF16)
    acc = None
    for c in range(0, D_FF, FF_CHUNK):
        up = jnp.maximum(_dot(h2, wup_ref[:, c:c + FF_CHUNK]), 0.0)
        d = _dot((up * up).astype(BF16), wdn_ref[c:c + FF_CHUNK, :])
        acc = d if acc is None else acc + d
    x = x + acc
    if final:
        x = _rms(x, gf_ref[...])
    o_ref[...] = x


def _merge_ffn(x, fa, fb, fc, w, final_g, tm, final):
    n = x.shape[0]
    row = lambda w_: pl.BlockSpec((tm, w_), lambda i: (i, 0))
    return pl.pallas_call(
        functools.partial(_merge_ffn_kernel, final=final),
        grid=(n // tm,),
        in_specs=[row(D_MODEL), row(CA), row(DB), row(DC),
                  _const_spec((1, D_MODEL)), _const_spec((D_MODEL, 3 * D_MODEL)),
                  _const_spec((CA, D_MODEL)), _const_spec((DB, D_MODEL)), _const_spec((DC, D_MODEL)),
                  _const_spec((D_MODEL, D_MODEL)), _const_spec((1, D_MODEL)),
                  _const_spec((D_MODEL, D_FF)), _const_spec((D_FF, D_MODEL)), _const_spec((1, D_MODEL))],
        out_specs=row(D_MODEL),
        out_shape=jax.ShapeDtypeStruct((n, D_MODEL), F32),
        compiler_params=pltpu.CompilerParams(dimension_semantics=("parallel",),
                                             vmem_limit_bytes=VMEM_LIMIT),
        name="merge_ffn",
    )(x, fa, fb, fc, w["g1"], w["w_gate"], w["w_a_out"], w["w_b_out"], w["w_c_out"], w["w_o"],
      w["g2"], w["w_up"], w["w_down"], final_g)


def _layer_weights(l, p, lbs):
    w_in = p["w_in"][l]
    cols = lambda a, b: w_in[:, a:b]
    w_a = jnp.concatenate([cols(O_AV, O_QB), cols(O_QB, O_FB), cols(O_QC, O_GA),
                           jnp.pad(cols(O_FB, O_QC), ((0, 0), (0, LANES - HB)))], axis=1).astype(BF16)
    lb = lbs[l]
    row = lambda v: v.reshape(1, -1).astype(F32)
    lbrows = jnp.concatenate([jnp.log(lb)[None], jnp.log1p(-lb)[None], (1.0 - lb)[None],
                              jnp.zeros((5, HC * DK), F32)], axis=0)
    return dict(
        w_a=w_a, g1=row(p["norm1_g"][l]),
        fbias=jnp.pad(p["fox_bf"][l], (0, LANES - HB)).reshape(1, LANES).astype(F32),
        lbrows=lbrows,
        conv_w=jnp.pad(p["conv_w"][l], ((0, HIST_ROWS - CONV_W), (0, 0))),
        conv_b=row(p["conv_b"][l]), ln_g=row(p["conv_ln_g"][l]), ln_b=row(p["conv_ln_b"][l]),
        hn_g=row(p["hgrn_norm_g"][l]),
        w_gate=cols(O_GA, O_END).astype(BF16),
        w_a_out=p["w_a_out"][l].astype(BF16), w_b_out=p["w_b_out"][l].astype(BF16),
        w_c_out=p["w_c_out"][l].astype(BF16), w_o=p["w_o"][l].astype(BF16),
        g2=row(p["norm2_g"][l]), w_up=p["w_up"][l].astype(BF16), w_down=p["w_down"][l].astype(BF16),
    )


def _pad_hist(h):
    return jnp.pad(h, ((0, 0), (HIST_PAD, 0), (0, 0)))


def _trunk_layer(x, w, final_g, final, conv_hist, s0, cache, *, tm_a, tm_d, tl, tq, hg_tile):
    bsz, L, _ = x.shape
    n = bsz * L
    xf = x.reshape(n, D_MODEL)
    u, qs, k, kb, v, vt, lf, bias, qh, lfc, kk, vv, og = _in_proj(
        xf, w["g1"], w["w_a"], w["fbias"], w["lbrows"], tm_a, max(L // tm_a, 1))
    seq = lambda a: a.reshape(bsz, L, a.shape[-1])

    feat, conv_new = _conv(seq(u), _pad_hist(conv_hist), w["conv_w"], w["conv_b"], w["ln_g"], w["ln_b"], tl)

    if cache is None:
        o_b = _attn_prompt(qs, kb, vt, bias, bsz, L, tq)
    else:
        k_cache_t, v_cache_t, layer, lf_cache = cache
        P = k_cache_t.shape[-1]
        lf_t = seq(lf).transpose(0, 2, 1)
        lf_all = jnp.concatenate([lf_cache.transpose(0, 2, 1), lf_t,
                                  jnp.zeros((bsz, HB, LANES - L), F32)], axis=-1)
        hi, mid, lo = _cumsum_bias(lf_all, pivot=P - 1)
        o_b = _attn_cached(seq(qs), k_cache_t, v_cache_t, layer,
                           seq(k), seq(v), hi, mid, lo).reshape(n, DB)

    C = min(HGRN_CHUNK, L)
    hg, s_new = _hgrn(seq(qh), seq(lfc), seq(kk), seq(vv), seq(og), s0, w["hn_g"], C, hg_tile)

    x_new = _merge_ffn(xf, feat.reshape(n, CA), o_b, hg.reshape(n, DC), w, final_g, tm_d, final)
    return (x_new.reshape(bsz, L, D_MODEL), seq(k).reshape(bsz, L, HB, DHB), seq(v).reshape(bsz, L, HB, DHB),
            seq(lf), conv_new[:, HIST_PAD:], s_new)


def _lower_bounds(p):
    s = jax.nn.softmax(p.astype(F32), axis=0)
    return jnp.maximum(jnp.cumsum(s, axis=0) - s[0], 0.0)


def kernel(x_prompt, x_sample, cache_fox_k, cache_fox_v, cache_fox_logf, state_conv, state_hgrn,
           norm1_g, w_in, conv_w, conv_b, conv_ln_g, conv_ln_b, w_a_out, fox_bf, w_b_out,
           hgrn_lb_param, hgrn_norm_g, w_c_out, w_o, norm2_g, w_up, w_down, final_g):
    p = dict(norm1_g=norm1_g, w_in=w_in, conv_w=conv_w, conv_b=conv_b, conv_ln_g=conv_ln_g,
             conv_ln_b=conv_ln_b, w_a_out=w_a_out, fox_bf=fox_bf, w_b_out=w_b_out,
             hgrn_norm_g=hgrn_norm_g, w_c_out=w_c_out, w_o=w_o, norm2_g=norm2_g, w_up=w_up, w_down=w_down)
    depth = w_in.shape[0]
    lbs = _lower_bounds(hgrn_lb_param)
    fg = final_g.reshape(1, D_MODEL).astype(F32)
    xp, xs = x_prompt, x_sample
    bp, sp, _ = xp.shape
    bs, ls, _ = xs.shape
    zero_hist = jnp.zeros((bp, CONV_W - 1, CA), F32)
    zero_s = jnp.zeros((bp, HC, DK, DV), F32)
    past = cache_fox_k.shape[2]
    k_cache_t = cache_fox_k.transpose(0, 1, 3, 4, 2).reshape(depth, bs, DB, past)
    v_cache_t = cache_fox_v.transpose(0, 1, 3, 4, 2).reshape(depth, bs, DB, past)
    outs_p, outs_s = [], []
    for l in range(depth):
        w = _layer_weights(l, p, lbs)
        final = l == depth - 1
        rp = _trunk_layer(xp, w, fg, final, zero_hist, zero_s, None,
                          tm_a=min(512, bp * sp), tm_d=min(256, bp * sp), tl=min(512, sp),
                          tq=min(512, sp), hg_tile=min(512, sp))
        rs = _trunk_layer(xs, w, fg, final, state_conv[l], state_hgrn[l],
                          (k_cache_t, v_cache_t, l, cache_fox_logf[l]),
                          tm_a=min(512, bs * ls), tm_d=min(256, bs * ls), tl=ls, tq=None, hg_tile=ls)
        xp, xs = rp[0], rs[0]
        outs_p.append(rp[1:])
        outs_s.append(rs[1:])
    stack = lambda outs, i: jnp.stack([o[i] for o in outs])
    return (xp, xs,
            stack(outs_p, 0), stack(outs_p, 1), stack(outs_p, 2), stack(outs_p, 3), stack(outs_p, 4),
            stack(outs_s, 0), stack(outs_s, 1), stack(outs_s, 2), stack(outs_s, 3), stack(outs_s, 4))
```

```python
import functools
import math

import numpy as np
import jax
import jax.numpy as jnp
from jax import lax
from jax.experimental import pallas as pl
from jax.experimental.pallas import tpu as pltpu

D_MODEL = 1024
CONV_W = 31
CA = 512
HB = 8
DHB = 64
DB = HB * DHB
HC = 4
DK = 128
DV = 128
DC = HC * DV
D_FF = 4 * D_MODEL
EPS = 1e-6
HGRN_CHUNK = 128

LANES = 128
SUBLANES = 8
VMEM_LIMIT = 56 * 1024 * 1024
NEG_BIG = -1e30
LOG2E = 1.4426950408889634

F32 = jnp.float32
BF16 = jnp.bfloat16

_OFF = np.cumsum([0, CA, CA, DB, DB, DB, HB, HC * DK, HC * DK, DC, DC, D_MODEL, D_MODEL, D_MODEL])
(O_AV, O_AG, O_QB, O_KB, O_VB, O_FB, O_QC, O_FC, O_IC, O_OC, O_GA, O_GB, O_GC, O_END) = [int(v) for v in _OFF]


def _dot(a, b):
    return jnp.dot(a, b, preferred_element_type=F32)


def _dot_nt(a, b):
    return lax.dot_general(a, b, (((1,), (1,)), ((), ())), preferred_element_type=F32)


def _dot_tn(a, b):
    return lax.dot_general(a, b, (((0,), (0,)), ((), ())), preferred_element_type=F32)


def _sigmoid(x):
    return 1.0 / (1.0 + jnp.exp(-x))


def _silu(x):
    return x * _sigmoid(x)


def _rms(x, g):
    return x * lax.rsqrt(jnp.mean(x * x, axis=-1, keepdims=True) + EPS) * g


def _split3(x):
    hi = x.astype(BF16)
    r = x - hi.astype(F32)
    mid = r.astype(BF16)
    lo = (r - mid.astype(F32)).astype(BF16)
    return hi, mid, lo


def _const_spec(shape):
    nd = len(shape)
    return pl.BlockSpec(shape, lambda *_: (0,) * nd, pipeline_mode=pl.Buffered(1))


def _layer_spec(shape, layer):
    nd = len(shape)
    return pl.BlockSpec((None,) + tuple(shape), lambda *_: (layer,) + (0,) * nd, pipeline_mode=pl.Buffered(1))


A_WIDTH = 9 * 512 + LANES


def _in_proj_kernel(x_ref, g1_ref, w_ref, fbias_ref, lb_ref, tri_ref,
                    u_ref, qs_ref, k_ref, kb_ref, v_ref, vt_ref, lf_ref, bias_ref,
                    qh_ref, lfc_ref, kk_ref, vv_ref, og_ref, carry_ref, *, tiles_per_seq):
    i = pl.program_id(0)

    @pl.when(i % tiles_per_seq == 0)
    def _():
        carry_ref[...] = jnp.zeros_like(carry_ref)

    x = x_ref[...]
    h = _rms(x, g1_ref[...]).astype(BF16)

    def seg(i, width=512):
        return _dot(h, w_ref[:, i * 512:i * 512 + width])

    f = seg(9, LANES) + fbias_ref[...]
    lf = jnp.minimum(f, 0.0) - jnp.log(1.0 + jnp.exp(-jnp.abs(f)))
    lf_ref[...] = lf[:, :HB]

    hi, mid, lo = _split3(lf)
    c3 = _dot(tri_ref[...], jnp.concatenate([hi, mid, lo], axis=1))
    c = (c3[:, :LANES] + c3[:, LANES:2 * LANES]) + c3[:, 2 * LANES:] + carry_ref[0:1, :]
    carry_ref[0:1, :] = c[c.shape[0] - 1:, :]
    kbias = -LOG2E * c
    for g in range(HB // HEAD_GROUP):
        bias_ref[g] = kbias[:, g * HEAD_GROUP:(g + 1) * HEAD_GROUP]

    u_ref[...] = seg(0) * _sigmoid(seg(1))
    qs_ref[...] = (seg(2) * (LOG2E * DHB ** -0.5)).astype(BF16)
    k = seg(3)
    k_ref[...] = k
    kb_ref[...] = k.astype(BF16)
    v = seg(4)
    v_ref[...] = v
    vt_ref[...] = v.T.astype(BF16)
    qh_ref[...] = _silu(seg(5)).astype(BF16)

    z = seg(6)
    log_lb = lb_ref[0:1, :]
    log1m_lb = lb_ref[1:2, :]
    one_m_lb = lb_ref[2:3, :]
    e = jnp.exp(-jnp.abs(z))
    ls = jnp.minimum(z, 0.0) - jnp.log(1.0 + e)
    b = log1m_lb + ls
    mx = jnp.maximum(log_lb, b)
    lfc_ref[...] = mx + jnp.log(1.0 + jnp.exp(-jnp.abs(log_lb - b)))
    r = 1.0 / (1.0 + e)
    kk_ref[...] = (one_m_lb * jnp.where(z >= 0.0, e * r, r)).astype(BF16)

    vv_ref[...] = seg(7).astype(BF16)
    og_ref[...] = _silu(seg(8)).astype(BF16)


def _in_proj(x, w, layer, tm, tiles_per_seq):
    n = x.shape[0]
    tok = lambda w_, dt: jax.ShapeDtypeStruct((n, w_), dt)
    row = lambda w_: pl.BlockSpec((tm, w_), lambda i: (i, 0))
    ng = HB // HEAD_GROUP
    tri = jnp.asarray(np.tril(np.ones((tm, tm), np.float32)), BF16)
    out_shape = (tok(512, F32), tok(512, BF16), tok(512, F32), tok(512, BF16), tok(512, F32),
                 jax.ShapeDtypeStruct((n // tm, DB, tm), BF16), tok(HB, F32),
                 jax.ShapeDtypeStruct((ng, n, HEAD_GROUP), F32),
                 tok(512, BF16), tok(512, F32), tok(512, BF16), tok(512, BF16), tok(512, BF16))
    out_specs = (row(512), row(512), row(512), row(512), row(512),
                 pl.BlockSpec((None, DB, tm), lambda i: (i, 0, 0)), row(HB),
                 pl.BlockSpec((ng, tm, HEAD_GROUP), lambda i: (0, i, 0)),
                 row(512), row(512), row(512), row(512), row(512))
    return pl.pallas_call(
        functools.partial(_in_proj_kernel, tiles_per_seq=tiles_per_seq),
        grid=(n // tm,),
        in_specs=[row(D_MODEL), _layer_spec((1, D_MODEL), layer), _layer_spec((D_MODEL, A_WIDTH), layer),
                  _layer_spec((1, LANES), layer), _layer_spec((8, 512), layer), _const_spec((tm, tm))],
        out_specs=out_specs,
        out_shape=out_shape,
        scratch_shapes=[pltpu.VMEM((8, LANES), F32)],
        compiler_params=pltpu.CompilerParams(dimension_semantics=("arbitrary",),
                                             vmem_limit_bytes=VMEM_LIMIT),
        name="in_proj",
    )(x, w["g1"], w["w_a"], w["fbias"], w["lbrows"], tri)


def _cumsum_kernel(lf_ref, hi_ref, mid_ref, lo_ref, *, n, pivot):
    x = lf_ref[...]
    lane = lax.broadcasted_iota(jnp.int32, x.shape, 1)
    s = 1
    while s < n:
        x = x + jnp.where(lane >= s, pltpu.roll(x, s, 1), 0.0)
        s *= 2
    piv = jnp.sum(jnp.where(lane == pivot, x, 0.0), axis=1, keepdims=True)
    hi, mid, lo = _split3(LOG2E * (piv - x))
    hi_ref[...] = hi
    mid_ref[...] = mid
    lo_ref[...] = lo


def _cumsum_bias(lf_t, pivot):
    bsz, hh, n = lf_t.shape
    spec = pl.BlockSpec((bsz * hh, n), lambda i: (0, 0))
    sds = jax.ShapeDtypeStruct((bsz * hh, n), BF16)
    outs = pl.pallas_call(
        functools.partial(_cumsum_kernel, n=n, pivot=pivot),
        grid=(1,),
        in_specs=[spec],
        out_specs=(spec, spec, spec),
        out_shape=(sds, sds, sds),
        name="cumsum_bias",
    )(lf_t.reshape(bsz * hh, n))
    return tuple(o.reshape(bsz, hh, n) for o in outs)


HIST_ROWS = 32
HIST_PAD = HIST_ROWS - (CONV_W - 1)


def _conv_kernel(u_ref, hist_ref, w_ref, cb_ref, g_ref, b_ref, feat_ref, new_ref, buf_ref, *, tl, rc):
    t = pl.program_id(1)

    @pl.when(t == 0)
    def _():
        buf_ref[0:HIST_ROWS, :] = hist_ref[0]

    @pl.when(t > 0)
    def _():
        buf_ref[0:HIST_ROWS, :] = buf_ref[tl:tl + HIST_ROWS, :]

    buf_ref[HIST_ROWS:HIST_ROWS + tl, :] = u_ref[0]
    new_ref[0] = buf_ref[tl:tl + HIST_ROWS, :]

    cb = cb_ref[...]
    g = g_ref[...]
    b = b_ref[...]
    for r0 in range(0, tl, rc):
        y = cb
        for r in range(SUBLANES):
            z = None
            for a in range((HIST_ROWS + SUBLANES) // SUBLANES):
                j = SUBLANES * a + r - HIST_PAD
                if 0 <= j < CONV_W:
                    nrows = rc + (SUBLANES if r else 0)
                    term = w_ref[j:j + 1, :] * buf_ref[r0 + SUBLANES * a:r0 + SUBLANES * a + nrows, :]
                    z = term if z is None else z + term
            y = y + z[r:r + rc]
        mu = jnp.mean(y, axis=-1, keepdims=True)
        yc = y - mu
        yn = yc * lax.rsqrt(jnp.mean(yc * yc, axis=-1, keepdims=True) + EPS) * g + b
        feat_ref[0, r0:r0 + rc, :] = _silu(yn).astype(BF16)


def _conv(u, hist, w, layer, tl):
    bsz, L, _ = u.shape
    rc = min(tl, 64)
    vec = _layer_spec((1, CA), layer)
    return pl.pallas_call(
        functools.partial(_conv_kernel, tl=tl, rc=rc),
        grid=(bsz, L // tl),
        in_specs=[pl.BlockSpec((1, tl, CA), lambda i, t: (i, t, 0)),
                  pl.BlockSpec((1, HIST_ROWS, CA), lambda i, t: (i, 0, 0)),
                  _layer_spec((HIST_ROWS, CA), layer), vec, vec, vec],
        out_specs=(pl.BlockSpec((1, tl, CA), lambda i, t: (i, t, 0)),
                   pl.BlockSpec((1, HIST_ROWS, CA), lambda i, t: (i, 0, 0))),
        out_shape=(jax.ShapeDtypeStruct((bsz, L, CA), BF16),
                   jax.ShapeDtypeStruct((bsz, HIST_ROWS, CA), F32)),
        scratch_shapes=[pltpu.VMEM((tl + HIST_ROWS, CA), F32)],
        compiler_params=pltpu.CompilerParams(dimension_semantics=("parallel", "arbitrary")),
        name="conv",
    )(u, hist, w["conv_w"], w["conv_b"], w["ln_g"], w["ln_b"])


ONES_ROWS = 16
HEAD_GROUP = 4
GROUP_LANES = HEAD_GROUP * DHB


def _attn_kernel(q_ref, k_ref, vt_ref, bias_ref, o_ref, *, tq):
    i = pl.program_id(1)
    ng = HB // HEAD_GROUP
    lane_head = lax.broadcasted_iota(jnp.int32, (tq, GROUP_LANES), 1) // DHB
    qm = []
    for g in range(ng):
        q = q_ref[:, g * GROUP_LANES:(g + 1) * GROUP_LANES]
        qm += [jnp.where(lane_head == h, q, jnp.zeros_like(q)) for h in range(HEAD_GROUP)]
    ones = jnp.ones((ONES_ROWS, tq), BF16)

    def step(j, carry, diag):
        rows = pl.ds(pl.multiple_of(j * tq, tq), tq)
        if diag:
            kk = lax.broadcasted_iota(jnp.int32, (tq, tq), 0)
            qq = lax.broadcasted_iota(jnp.int32, (tq, tq), 1)
            keep = kk <= qq
        ss = []
        for g in range(ng):
            kblk = k_ref[rows, g * GROUP_LANES:(g + 1) * GROUP_LANES]
            for h in range(HEAD_GROUP):
                hh = g * HEAD_GROUP + h
                ss.append(_dot_nt(kblk, qm[hh]) + bias_ref[g, rows, h:h + 1])
        if diag:
            ss = [jnp.where(keep, s, NEG_BIG) for s in ss]
        ms = [jnp.maximum(carry[h][0], jnp.max(ss[h], axis=0, keepdims=True)) for h in range(HB)]
        out = []
        for h in range(HB):
            m, acc = carry[h]
            alpha = jnp.exp2(m - ms[h])
            p = jnp.exp2(ss[h] - ms[h]).astype(BF16)
            vt = jnp.concatenate([vt_ref[j, h * DHB:(h + 1) * DHB, :], ones], axis=0)
            out.append((ms[h], acc * alpha + _dot(vt, p)))
        return tuple(out)

    init = tuple((jnp.full((1, tq), NEG_BIG, F32), jnp.zeros((DHB + ONES_ROWS, tq), F32))
                 for _ in range(HB))
    carry = lax.fori_loop(0, i, lambda j, c: step(j, c, False), init)
    carry = step(i, carry, True)
    o_t = jnp.concatenate([acc[:DHB] / acc[DHB:DHB + 1] for _, acc in carry], axis=0)
    o_ref[...] = o_t.T.astype(BF16)


def _attn_prompt(qs, kb, vt, bias, bsz, S, tq):
    nq = S // tq
    ng = HB // HEAD_GROUP
    once = pl.Buffered(1)
    return pl.pallas_call(
        functools.partial(_attn_kernel, tq=tq),
        grid=(bsz, nq),
        in_specs=[pl.BlockSpec((tq, DB), lambda b, i: (b * nq + i, 0)),
                  pl.BlockSpec((S, DB), lambda b, i: (b, 0), pipeline_mode=once),
                  pl.BlockSpec((nq, DB, tq), lambda b, i: (b, 0, 0), pipeline_mode=once),
                  pl.BlockSpec((ng, S, HEAD_GROUP), lambda b, i: (0, b, 0), pipeline_mode=once)],
        out_specs=pl.BlockSpec((tq, DB), lambda b, i: (b * nq + i, 0)),
        out_shape=jax.ShapeDtypeStruct((bsz * S, DB), BF16),
        compiler_params=pltpu.CompilerParams(dimension_semantics=("parallel", "arbitrary"),
                                             vmem_limit_bytes=VMEM_LIMIT),
        name="attn_prompt",
    )(qs, kb, vt, bias)


def _attn_cached_kernel(q_ref, kc_ref, vc_ref, kn_ref, vn_ref, bhi_ref, bmid_ref, blo_ref, o_ref, *, L, P):
    q = q_ref[0]
    lane_head = lax.broadcasted_iota(jnp.int32, (L, DB), 1) // DHB
    q_bd = jnp.concatenate([jnp.where(lane_head == h, q, jnp.zeros_like(q)) for h in range(HB)], axis=0)
    kc = kc_ref[...].astype(BF16)
    vc = vc_ref[...].astype(BF16)
    zpad = jnp.zeros((LANES - L, DB), BF16)
    kn = jnp.concatenate([kn_ref[0].astype(BF16), zpad], axis=0)
    vn = jnp.concatenate([vn_ref[0].astype(BF16), zpad], axis=0)
    bias = (bhi_ref[0].astype(F32) + bmid_ref[0].astype(F32)) + blo_ref[0].astype(F32)

    s_c = _dot(q_bd, kc)
    s_n = _dot_nt(q_bd, kn)
    tt = lax.broadcasted_iota(jnp.int32, (L, LANES), 0)
    uu = lax.broadcasted_iota(jnp.int32, (L, LANES), 1)
    causal = uu <= tt
    pcs, pns, ls = [], [], []
    for h in range(HB):
        sc = s_c[h * L:(h + 1) * L, :] + bias[h:h + 1, :P]
        sn = jnp.where(causal, s_n[h * L:(h + 1) * L, :] + bias[h:h + 1, P:], NEG_BIG)
        m = jnp.maximum(jnp.max(sc, axis=1, keepdims=True), jnp.max(sn, axis=1, keepdims=True))
        pc = jnp.exp2(sc - m)
        pn = jnp.exp2(sn - m)
        ls.append(jnp.sum(pc, axis=1, keepdims=True) + jnp.sum(pn, axis=1, keepdims=True))
        pcs.append(pc.astype(BF16))
        pns.append(pn.astype(BF16))
    o_all = _dot_nt(jnp.concatenate(pcs, axis=0), vc) + _dot(jnp.concatenate(pns, axis=0), vn)
    out = jnp.zeros((L, DB), F32)
    for h in range(HB):
        out = out + jnp.where(lane_head == h, o_all[h * L:(h + 1) * L, :] / ls[h], 0.0)
    o_ref[0] = out.astype(BF16)


def _attn_cached(qs, k_cache_t, v_cache_t, layer, k_new, v_new, bhi, bmid, blo):
    bsz, L, _ = qs.shape
    P = k_cache_t.shape[-1]
    seq = lambda r, w: pl.BlockSpec((1, r, w), lambda b: (b, 0, 0))
    cache = pl.BlockSpec((None, None, DB, P), lambda b: (layer, b, 0, 0))
    return pl.pallas_call(
        functools.partial(_attn_cached_kernel, L=L, P=P),
        grid=(bsz,),
        in_specs=[seq(L, DB), cache, cache, seq(L, DB), seq(L, DB),
                  seq(HB, P + LANES), seq(HB, P + LANES), seq(HB, P + LANES)],
        out_specs=seq(L, DB),
        out_shape=jax.ShapeDtypeStruct((bsz, L, DB), BF16),
        compiler_params=pltpu.CompilerParams(dimension_semantics=("parallel",),
                                             vmem_limit_bytes=VMEM_LIMIT),
        name="attn_cached",
    )(qs, k_cache_t, v_cache_t, k_new, v_new, bhi, bmid, blo)


def _hgrn_exponent_matrix(C):
    nl = int(math.log2(C))
    rows = []
    idx = np.arange(C)
    for lv in range(nl):
        sz = 1 << lv
        m = np.zeros((C, C), np.float32)
        for t in range(C):
            bnd = ((t >> (lv + 1)) << (lv + 1)) + sz - 1
            if (t >> lv) & 1:
                m[t, (idx > bnd) & (idx <= t)] = 1.0
            else:
                m[t, (idx > t) & (idx <= bnd)] = 1.0
        rows.append(m)
    rows.append((idx[None, :] <= idx[:, None]).astype(np.float32))
    rows.append((idx[None, :] > idx[:, None]).astype(np.float32))
    p = np.concatenate(rows, axis=0)
    return np.concatenate([p, p], axis=1)


def _hgrn_kernel(p_ref, qh_ref, lf_ref, kk_ref, vv_ref, og_ref, s0_ref, g_ref, hg_ref, sout_ref, st_ref,
                 *, C, T):
    t = pl.program_id(1)
    nl = int(math.log2(C))

    @pl.when(t == 0)
    def _():
        for h in range(HC):
            st_ref[h] = s0_ref[0, h].T

    row = lax.broadcasted_iota(jnp.int32, (C, C), 0)
    col = lax.broadcasted_iota(jnp.int32, (C, C), 1)
    diff = row ^ col
    masks = [(jnp.right_shift(diff, lv) == 1) & ((jnp.right_shift(row, lv) & 1) == 1) for lv in range(nl)]
    diag = row == col
    g = g_ref[...]
    pmat = p_ref[...]

    pre = {}
    for c0 in range(0, T, C):
        lf = LOG2E * lf_ref[0, c0:c0 + C, :]
        hi = lf.astype(BF16)
        lo = (lf - hi.astype(F32)).astype(BF16)
        e_all = jnp.exp2(_dot(pmat, jnp.concatenate([hi, lo], axis=0)))
        for h in range(HC):
            hs = slice(h * DK, (h + 1) * DK)
            qb = qh_ref[0, c0:c0 + C, hs]
            kb = kk_ref[0, c0:c0 + C, hs]
            q = qb.astype(F32)
            k = kb.astype(F32)
            a = jnp.where(diag, _dot_nt(qb, kb), 0.0)
            for lv in range(nl):
                e = e_all[lv * C:(lv + 1) * C, hs]
                a = jnp.where(masks[lv], _dot_nt((q * e).astype(BF16), (k * e).astype(BF16)), a)
            e_q = e_all[nl * C:(nl + 1) * C, hs]
            e_k = e_all[(nl + 1) * C:(nl + 2) * C, hs]
            pre[c0, h] = (a.astype(BF16), (q * e_q).astype(BF16), (k * e_k).astype(BF16), e_q[C - 1:C, :])

    for h in range(HC):
        hs = slice(h * DK, (h + 1) * DK)
        st = st_ref[h]
        for c0 in range(0, T, C):
            a, q_dec, k_dec, chunk_dec = pre[c0, h]
            v = vv_ref[0, c0:c0 + C, hs]
            o = _dot(a, v) + _dot_nt(q_dec, st.astype(BF16))
            st = st * chunk_dec + _dot_tn(v, k_dec)
            o = o * lax.rsqrt(jnp.mean(o * o, axis=-1, keepdims=True) + EPS) * g
            hg_ref[0, c0:c0 + C, hs] = (o * og_ref[0, c0:c0 + C, hs].astype(F32)).astype(BF16)
        st_ref[h] = st

    @pl.when(t == pl.num_programs(1) - 1)
    def _():
        for h in range(HC):
            sout_ref[0, h] = st_ref[h].T


def _hgrn(qh, lfc, kk, vv, og, s0, g, layer, C, T):
    bsz, L, _ = qh.shape
    pmat = jnp.asarray(_hgrn_exponent_matrix(C), BF16)
    tile = pl.BlockSpec((1, T, DC), lambda b, t: (b, t, 0))
    state = pl.BlockSpec((1, HC, DK, DV), lambda b, t: (b, 0, 0, 0))
    return pl.pallas_call(
        functools.partial(_hgrn_kernel, C=C, T=T),
        grid=(bsz, L // T),
        in_specs=[_const_spec(pmat.shape), tile, tile, tile, tile, tile, state, _layer_spec((1, DV), layer)],
        out_specs=(tile, state),
        out_shape=(jax.ShapeDtypeStruct((bsz, L, DC), BF16),
                   jax.ShapeDtypeStruct((bsz, HC, DK, DV), F32)),
        scratch_shapes=[pltpu.VMEM((HC, DV, DK), F32)],
        compiler_params=pltpu.CompilerParams(dimension_semantics=("parallel", "arbitrary")),
        name="hgrn",
    )(pmat, qh, lfc, kk, vv, og, s0, g)


FF_CHUNK = 1024


def _merge_ffn_kernel(x_ref, fa_ref, fb_ref, fc_ref, g1_ref, wg_ref, wa_ref, wb_ref, wc_ref, wo_ref,
                      g2_ref, wup_ref, wdn_ref, gf_ref, o_ref, *, final):
    x = x_ref[...]
    h1 = _rms(x, g1_ref[...]).astype(BF16)
    m = None
    for i, (f_ref, w_ref) in enumerate(((fa_ref, wa_ref), (fb_ref, wb_ref), (fc_ref, wc_ref))):
        gate = _sigmoid(_dot(h1, wg_ref[:, i * D_MODEL:(i + 1) * D_MODEL]))
        y = gate * _dot(f_ref[...], w_ref[...])
        m = y if m is None else m + y
    x = x + _dot(m.astype(BF16), wo_ref[...])
    h2 = _rms(x, g2_ref[...]).astype(BF16)
    acc = None
    for c in range(0, D_FF, FF_CHUNK):
        up = jnp.maximum(_dot(h2, wup_ref[:, c:c + FF_CHUNK]), 0.0)
        d = _dot((up * up).astype(BF16), wdn_ref[c:c + FF_CHUNK, :])
        acc = d if acc is None else acc + d
    x = x + acc
    if final:
        x = _rms(x, gf_ref[...])
    o_ref[...] = x


def _merge_ffn(x, fa, fb, fc, w, layer, final_g, tm, final):
    n = x.shape[0]
    row = lambda w_: pl.BlockSpec((tm, w_), lambda i: (i, 0))
    lw = lambda *shape: _layer_spec(shape, layer)
    return pl.pallas_call(
        functools.partial(_merge_ffn_kernel, final=final),
        grid=(n // tm,),
        in_specs=[row(D_MODEL), row(CA), row(DB), row(DC),
                  lw(1, D_MODEL), lw(D_MODEL, 3 * D_MODEL),
                  lw(CA, D_MODEL), lw(DB, D_MODEL), lw(DC, D_MODEL),
                  lw(D_MODEL, D_MODEL), lw(1, D_MODEL),
                  lw(D_MODEL, D_FF), lw(D_FF, D_MODEL), _const_spec((1, D_MODEL))],
        out_specs=row(D_MODEL),
        out_shape=jax.ShapeDtypeStruct((n, D_MODEL), F32),
        compiler_params=pltpu.CompilerParams(dimension_semantics=("parallel",),
                                             vmem_limit_bytes=VMEM_LIMIT),
        name="merge_ffn",
    )(x, fa, fb, fc, w["g1"], w["w_gate"], w["w_a_out"], w["w_b_out"], w["w_c_out"], w["w_o"],
      w["g2"], w["w_up"], w["w_down"], final_g)


def _stacked_weights(p, lbs):
    w_in = p["w_in"]
    cols = lambda a, b: w_in[:, :, a:b]
    w_a = jnp.concatenate([cols(O_AV, O_QB), cols(O_QB, O_FB), cols(O_QC, O_GA),
                           jnp.pad(cols(O_FB, O_QC), ((0, 0), (0, 0), (0, LANES - HB)))], axis=2).astype(BF16)
    row = lambda v: v[:, None, :].astype(F32)
    lbrows = jnp.concatenate([jnp.log(lbs)[:, None], jnp.log1p(-lbs)[:, None], (1.0 - lbs)[:, None],
                              jnp.zeros((lbs.shape[0], 5, HC * DK), F32)], axis=1)
    return dict(
        w_a=w_a, g1=row(p["norm1_g"]),
        fbias=row(jnp.pad(p["fox_bf"], ((0, 0), (0, LANES - HB)))),
        lbrows=lbrows,
        conv_w=jnp.pad(p["conv_w"], ((0, 0), (0, HIST_ROWS - CONV_W), (0, 0))),
        conv_b=row(p["conv_b"]), ln_g=row(p["conv_ln_g"]), ln_b=row(p["conv_ln_b"]),
        hn_g=row(p["hgrn_norm_g"]),
        w_gate=cols(O_GA, O_END).astype(BF16),
        w_a_out=p["w_a_out"].astype(BF16), w_b_out=p["w_b_out"].astype(BF16),
        w_c_out=p["w_c_out"].astype(BF16), w_o=p["w_o"].astype(BF16),
        g2=row(p["norm2_g"]), w_up=p["w_up"].astype(BF16), w_down=p["w_down"].astype(BF16),
    )


def _pad_hist(h):
    return jnp.pad(h, ((0, 0), (HIST_PAD, 0), (0, 0)))


def _trunk_layer(x, w, layer, final_g, final, conv_hist, s0, cache, *, tm_a, tm_d, tl, tq, hg_tile):
    bsz, L, _ = x.shape
    n = bsz * L
    xf = x.reshape(n, D_MODEL)
    u, qs, k, kb, v, vt, lf, bias, qh, lfc, kk, vv, og = _in_proj(xf, w, layer, tm_a, max(L // tm_a, 1))
    seq = lambda a: a.reshape(bsz, L, a.shape[-1])

    feat, conv_new = _conv(seq(u), _pad_hist(conv_hist), w, layer, tl)

    if cache is None:
        o_b = _attn_prompt(qs, kb, vt, bias, bsz, L, tq)
    else:
        k_cache_t, v_cache_t, lf_cache = cache
        P = k_cache_t.shape[-1]
        lf_t = seq(lf).transpose(0, 2, 1)
        lf_all = jnp.concatenate([lf_cache.transpose(0, 2, 1), lf_t,
                                  jnp.zeros((bsz, HB, LANES - L), F32)], axis=-1)
        hi, mid, lo = _cumsum_bias(lf_all, pivot=P - 1)
        o_b = _attn_cached(seq(qs), k_cache_t, v_cache_t, layer,
                           seq(k), seq(v), hi, mid, lo).reshape(n, DB)

    C = min(HGRN_CHUNK, L)
    hg, s_new = _hgrn(seq(qh), seq(lfc), seq(kk), seq(vv), seq(og), s0, w["hn_g"], layer, C, hg_tile)

    x_new = _merge_ffn(xf, feat.reshape(n, CA), o_b, hg.reshape(n, DC), w, layer, final_g, tm_d, final)
    return (x_new.reshape(bsz, L, D_MODEL), seq(k).reshape(bsz, L, HB, DHB), seq(v).reshape(bsz, L, HB, DHB),
            seq(lf), conv_new[:, HIST_PAD:], s_new)


def _lower_bounds(p):
    s = jax.nn.softmax(p.astype(F32), axis=0)
    return jnp.maximum(jnp.cumsum(s, axis=0) - s[0], 0.0)


def kernel(x_prompt, x_sample, cache_fox_k, cache_fox_v, cache_fox_logf, state_conv, state_hgrn,
           norm1_g, w_in, conv_w, conv_b, conv_ln_g, conv_ln_b, w_a_out, fox_bf, w_b_out,
           hgrn_lb_param, hgrn_norm_g, w_c_out, w_o, norm2_g, w_up, w_down, final_g):
    p = dict(norm1_g=norm1_g, w_in=w_in, conv_w=conv_w, conv_b=conv_b, conv_ln_g=conv_ln_g,
             conv_ln_b=conv_ln_b, w_a_out=w_a_out, fox_bf=fox_bf, w_b_out=w_b_out,
             hgrn_norm_g=hgrn_norm_g, w_c_out=w_c_out, w_o=w_o, norm2_g=norm2_g, w_up=w_up, w_down=w_down)
    depth = w_in.shape[0]
    lbs = _lower_bounds(hgrn_lb_param)
    fg = final_g.reshape(1, D_MODEL).astype(F32)
    xp, xs = x_prompt, x_sample
    bp, sp, _ = xp.shape
    bs, ls, _ = xs.shape
    zero_hist = jnp.zeros((bp, CONV_W - 1, CA), F32)
    zero_s = jnp.zeros((bp, HC, DK, DV), F32)
    past = cache_fox_k.shape[2]
    k_cache_t = cache_fox_k.transpose(0, 1, 3, 4, 2).reshape(depth, bs, DB, past)
    v_cache_t = cache_fox_v.transpose(0, 1, 3, 4, 2).reshape(depth, bs, DB, past)
    outs_p, outs_s = [], []
    w = _stacked_weights(p, lbs)
    for l in range(depth):
        final = l == depth - 1
        rp = _trunk_layer(xp, w, l, fg, final, zero_hist, zero_s, None,
                          tm_a=min(512, bp * sp), tm_d=min(256, bp * sp), tl=min(512, sp),
                          tq=min(512, sp), hg_tile=min(512, sp))
        rs = _trunk_layer(xs, w, l, fg, final, state_conv[l], state_hgrn[l],
                          (k_cache_t, v_cache_t, cache_fox_logf[l]),
                          tm_a=min(512, bs * ls), tm_d=min(256, bs * ls), tl=ls, tq=None, hg_tile=ls)
        xp, xs = rp[0], rs[0]
        outs_p.append(rp[1:])
        outs_s.append(rs[1:])
    stack = lambda outs, i: jnp.stack([o[i] for o in outs])
    return (xp, xs,
            stack(outs_p, 0), stack(outs_p, 1), stack(outs_p, 2), stack(outs_p, 3), stack(outs_p, 4),
            stack(outs_s, 0), stack(outs_s, 1), stack(outs_s, 2), stack(outs_s, 3), stack(outs_s, 4))
```

```python
import functools
import math

import numpy as np
import jax
import jax.numpy as jnp
from jax import lax
from jax.experimental import pallas as pl
from jax.experimental.pallas import tpu as pltpu

D_MODEL = 1024
CONV_W = 31
CA = 512
HB = 8
DHB = 64
DB = HB * DHB
HC = 4
DK = 128
DV = 128
DC = HC * DV
D_FF = 4 * D_MODEL
EPS = 1e-6
HGRN_CHUNK = 128

LANES = 128
SUBLANES = 8
VMEM_LIMIT = 56 * 1024 * 1024
NEG_BIG = -1e30
LOG2E = 1.4426950408889634

F32 = jnp.float32
BF16 = jnp.bfloat16

_OFF = np.cumsum([0, CA, CA, DB, DB, DB, HB, HC * DK, HC * DK, DC, DC, D_MODEL, D_MODEL, D_MODEL])
(O_AV, O_AG, O_QB, O_KB, O_VB, O_FB, O_QC, O_FC, O_IC, O_OC, O_GA, O_GB, O_GC, O_END) = [int(v) for v in _OFF]


def _dot(a, b):
    return jnp.dot(a, b, preferred_element_type=F32)


def _dot_nt(a, b):
    return lax.dot_general(a, b, (((1,), (1,)), ((), ())), preferred_element_type=F32)


def _dot_tn(a, b):
    return lax.dot_general(a, b, (((0,), (0,)), ((), ())), preferred_element_type=F32)


def _sigmoid(x):
    return 1.0 / (1.0 + jnp.exp(-x))


def _silu(x):
    return x * _sigmoid(x)


def _rms(x, g):
    return x * lax.rsqrt(jnp.mean(x * x, axis=-1, keepdims=True) + EPS) * g


def _split3(x):
    hi = x.astype(BF16)
    r = x - hi.astype(F32)
    mid = r.astype(BF16)
    lo = (r - mid.astype(F32)).astype(BF16)
    return hi, mid, lo


def _const_spec(shape):
    nd = len(shape)
    return pl.BlockSpec(shape, lambda *_: (0,) * nd, pipeline_mode=pl.Buffered(1))


def _layer_spec(shape, layer):
    nd = len(shape)
    return pl.BlockSpec((None,) + tuple(shape), lambda *_: (layer,) + (0,) * nd, pipeline_mode=pl.Buffered(1))


A_WIDTH = 9 * 512 + LANES
PAIR_LANES = 2 * LANES
KAUG_WIDTH = (HB // 2) * PAIR_LANES


def _bias_placement_matrix():
    m = np.zeros((3 * LANES, (HB // 2) * LANES), np.float32)
    for head in range(HB):
        pair, e = divmod(head, 2)
        for part in range(3):
            m[part * LANES + head, pair * LANES + 3 * e + part] = 1.0
    return m


def _in_proj_kernel(x_ref, g1_ref, w_ref, fbias_ref, lb_ref, tri_ref, place_ref, *rest,
                    tiles_per_seq, n_alias):
    (u_ref, qs_ref, k_ref, kaug_ref, v_ref, vt_ref, lf_ref,
     qh_ref, lfc_ref, kk_ref, vv_ref, og_ref, carry_ref) = rest[n_alias:]
    i = pl.program_id(0)

    @pl.when(i % tiles_per_seq == 0)
    def _():
        carry_ref[...] = jnp.zeros_like(carry_ref)

    x = x_ref[...]
    h = _rms(x, g1_ref[...]).astype(BF16)

    def seg(i, width=512):
        return _dot(h, w_ref[:, i * 512:i * 512 + width])

    f = seg(9, LANES) + fbias_ref[...]
    lf = jnp.minimum(f, 0.0) - jnp.log(1.0 + jnp.exp(-jnp.abs(f)))
    lf_ref[...] = lf[:, :HB]

    hi, mid, lo = _split3(lf)
    c3 = _dot(tri_ref[...], jnp.concatenate([hi, mid, lo], axis=1))
    c = (c3[:, :LANES] + c3[:, LANES:2 * LANES]) + c3[:, 2 * LANES:] + carry_ref[0:1, :]
    carry_ref[0:1, :] = c[c.shape[0] - 1:, :]
    bh, bm, bl = _split3(-LOG2E * c)
    placed = _dot(jnp.concatenate([bh, bm, bl], axis=1), place_ref[...]).astype(BF16)

    u_ref[...] = seg(0) * _sigmoid(seg(1))
    qs_ref[...] = (seg(2) * (LOG2E * DHB ** -0.5)).astype(BF16)
    k = seg(3)
    k_ref[...] = k
    for p in range(HB // 2):
        kaug_ref[:, 2 * p * LANES:(2 * p + 1) * LANES] = k[:, p * LANES:(p + 1) * LANES].astype(BF16)
        kaug_ref[:, (2 * p + 1) * LANES:(2 * p + 2) * LANES] = placed[:, p * LANES:(p + 1) * LANES]
    v = seg(4)
    v_ref[...] = v
    vt_ref[...] = v.T.astype(BF16)
    qh_ref[...] = _silu(seg(5)).astype(BF16)

    z = seg(6)
    log_lb = lb_ref[0:1, :]
    log1m_lb = lb_ref[1:2, :]
    one_m_lb = lb_ref[2:3, :]
    e = jnp.exp(-jnp.abs(z))
    ls = jnp.minimum(z, 0.0) - jnp.log(1.0 + e)
    b = log1m_lb + ls
    mx = jnp.maximum(log_lb, b)
    lfc_ref[...] = mx + jnp.log(1.0 + jnp.exp(-jnp.abs(log_lb - b)))
    r = 1.0 / (1.0 + e)
    kk_ref[...] = (one_m_lb * jnp.where(z >= 0.0, e * r, r)).astype(BF16)

    vv_ref[...] = seg(7).astype(BF16)
    og_ref[...] = _silu(seg(8)).astype(BF16)


def _in_proj(x, w, layer, depth, stacks, tm, tiles_per_seq):
    n = x.shape[0]
    tok = lambda w_, dt: jax.ShapeDtypeStruct((n, w_), dt)
    row = lambda w_: pl.BlockSpec((tm, w_), lambda i: (i, 0))
    slab = lambda w_: pl.BlockSpec((None, tm, w_), lambda i: (layer, i, 0))
    stk = lambda w_: jax.ShapeDtypeStruct((depth, n, w_), F32)
    tri = jnp.asarray(np.tril(np.ones((tm, tm), np.float32)), BF16)
    place = jnp.asarray(_bias_placement_matrix(), BF16)
    out_shape = (tok(512, F32), tok(512, BF16), stk(512), tok(KAUG_WIDTH, BF16), stk(512),
                 jax.ShapeDtypeStruct((n // tm, DB, tm), BF16), stk(HB),
                 tok(512, BF16), tok(512, F32), tok(512, BF16), tok(512, BF16), tok(512, BF16))
    out_specs = (row(512), row(512), slab(512), row(KAUG_WIDTH), slab(512),
                 pl.BlockSpec((None, DB, tm), lambda i: (i, 0, 0)), slab(HB),
                 row(512), row(512), row(512), row(512), row(512))
    in_specs = [row(D_MODEL), _layer_spec((1, D_MODEL), layer), _layer_spec((D_MODEL, A_WIDTH), layer),
                _layer_spec((1, LANES), layer), _layer_spec((8, 512), layer), _const_spec((tm, tm)),
                _const_spec(place.shape)]
    args = [x, w["g1"], w["w_a"], w["fbias"], w["lbrows"], tri, place]
    aliases = {}
    if stacks is not None:
        aliases = {len(args) + a: out_idx for a, out_idx in enumerate((2, 4, 6))}
        in_specs += [pl.BlockSpec(memory_space=pl.ANY)] * len(stacks)
        args += list(stacks)
    return pl.pallas_call(
        functools.partial(_in_proj_kernel, tiles_per_seq=tiles_per_seq, n_alias=len(aliases)),
        grid=(n // tm,),
        in_specs=in_specs,
        out_specs=out_specs,
        out_shape=out_shape,
        input_output_aliases=aliases,
        scratch_shapes=[pltpu.VMEM((8, LANES), F32)],
        compiler_params=pltpu.CompilerParams(dimension_semantics=("arbitrary",),
                                             vmem_limit_bytes=VMEM_LIMIT),
        name="in_proj",
    )(*args)


def _cumsum_kernel(lf_ref, hi_ref, mid_ref, lo_ref, *, n, pivot):
    x = lf_ref[...]
    lane = lax.broadcasted_iota(jnp.int32, x.shape, 1)
    s = 1
    while s < n:
        x = x + jnp.where(lane >= s, pltpu.roll(x, s, 1), 0.0)
        s *= 2
    piv = jnp.sum(jnp.where(lane == pivot, x, 0.0), axis=1, keepdims=True)
    hi, mid, lo = _split3(LOG2E * (piv - x))
    hi_ref[...] = hi
    mid_ref[...] = mid
    lo_ref[...] = lo


def _cumsum_bias(lf_t, pivot):
    bsz, hh, n = lf_t.shape
    spec = pl.BlockSpec((bsz * hh, n), lambda i: (0, 0))
    sds = jax.ShapeDtypeStruct((bsz * hh, n), BF16)
    outs = pl.pallas_call(
        functools.partial(_cumsum_kernel, n=n, pivot=pivot),
        grid=(1,),
        in_specs=[spec],
        out_specs=(spec, spec, spec),
        out_shape=(sds, sds, sds),
        name="cumsum_bias",
    )(lf_t.reshape(bsz * hh, n))
    return tuple(o.reshape(bsz, hh, n) for o in outs)


HIST_ROWS = 32
HIST_PAD = HIST_ROWS - (CONV_W - 1)


def _conv_kernel(u_ref, hist_ref, w_ref, cb_ref, g_ref, b_ref, feat_ref, new_ref, buf_ref, *, tl, rc):
    t = pl.program_id(1)

    @pl.when(t == 0)
    def _():
        buf_ref[0:HIST_ROWS, :] = hist_ref[0]

    @pl.when(t > 0)
    def _():
        buf_ref[0:HIST_ROWS, :] = buf_ref[tl:tl + HIST_ROWS, :]

    buf_ref[HIST_ROWS:HIST_ROWS + tl, :] = u_ref[0]
    new_ref[0] = buf_ref[tl:tl + HIST_ROWS, :]

    cb = cb_ref[...]
    g = g_ref[...]
    b = b_ref[...]
    for r0 in range(0, tl, rc):
        y = cb
        for r in range(SUBLANES):
            z = None
            for a in range((HIST_ROWS + SUBLANES) // SUBLANES):
                j = SUBLANES * a + r - HIST_PAD
                if 0 <= j < CONV_W:
                    nrows = rc + (SUBLANES if r else 0)
                    term = w_ref[j:j + 1, :] * buf_ref[r0 + SUBLANES * a:r0 + SUBLANES * a + nrows, :]
                    z = term if z is None else z + term
            y = y + z[r:r + rc]
        mu = jnp.mean(y, axis=-1, keepdims=True)
        yc = y - mu
        yn = yc * lax.rsqrt(jnp.mean(yc * yc, axis=-1, keepdims=True) + EPS) * g + b
        feat_ref[0, r0:r0 + rc, :] = _silu(yn).astype(BF16)


def _conv(u, hist, w, layer, tl):
    bsz, L, _ = u.shape
    rc = min(tl, 64)
    vec = _layer_spec((1, CA), layer)
    return pl.pallas_call(
        functools.partial(_conv_kernel, tl=tl, rc=rc),
        grid=(bsz, L // tl),
        in_specs=[pl.BlockSpec((1, tl, CA), lambda i, t: (i, t, 0)),
                  pl.BlockSpec((1, HIST_ROWS, CA), lambda i, t: (i, 0, 0)),
                  _layer_spec((HIST_ROWS, CA), layer), vec, vec, vec],
        out_specs=(pl.BlockSpec((1, tl, CA), lambda i, t: (i, t, 0)),
                   pl.BlockSpec((1, HIST_ROWS, CA), lambda i, t: (i, 0, 0))),
        out_shape=(jax.ShapeDtypeStruct((bsz, L, CA), BF16),
                   jax.ShapeDtypeStruct((bsz, HIST_ROWS, CA), F32)),
        scratch_shapes=[pltpu.VMEM((tl + HIST_ROWS, CA), F32)],
        compiler_params=pltpu.CompilerParams(dimension_semantics=("parallel", "arbitrary")),
        name="conv",
    )(u, hist, w["conv_w"], w["conv_b"], w["ln_g"], w["ln_b"])


ONES_ROWS = 16


def _attn_kernel(q_ref, k_ref, vt_ref, o_ref, *, tq):
    i = pl.program_id(1)
    lane = lax.broadcasted_iota(jnp.int32, (tq, LANES), 1)
    qm = []
    for p in range(HB // 2):
        q = q_ref[:, p * LANES:(p + 1) * LANES]
        for e in range(2):
            qa = jnp.where(lane // DHB == e, q, jnp.zeros_like(q))
            pick = jnp.where((lane >= 3 * e) & (lane < 3 * e + 3), 1.0, 0.0).astype(BF16)
            qm.append(jnp.concatenate([qa, pick], axis=1))
    ones = jnp.ones((ONES_ROWS, tq), BF16)

    def step(j, carry, diag):
        rows = pl.ds(pl.multiple_of(j * tq, tq), tq)
        if diag:
            kk = lax.broadcasted_iota(jnp.int32, (tq, tq), 0)
            qq = lax.broadcasted_iota(jnp.int32, (tq, tq), 1)
            keep = kk <= qq
        ss = []
        for p in range(HB // 2):
            kblk = k_ref[rows, p * PAIR_LANES:(p + 1) * PAIR_LANES]
            for e in range(2):
                ss.append(_dot_nt(kblk, qm[2 * p + e]))
        if diag:
            ss = [jnp.where(keep, s, NEG_BIG) for s in ss]
        ms = [jnp.maximum(carry[h][0], jnp.max(ss[h], axis=0, keepdims=True)) for h in range(HB)]
        out = []
        for h in range(HB):
            m, acc = carry[h]
            alpha = jnp.exp2(m - ms[h])
            p = jnp.exp2(ss[h] - ms[h]).astype(BF16)
            vt = jnp.concatenate([vt_ref[j, h * DHB:(h + 1) * DHB, :], ones], axis=0)
            out.append((ms[h], acc * alpha + _dot(vt, p)))
        return tuple(out)

    init = tuple((jnp.full((1, tq), NEG_BIG, F32), jnp.zeros((DHB + ONES_ROWS, tq), F32))
                 for _ in range(HB))
    carry = lax.fori_loop(0, i, lambda j, c: step(j, c, False), init)
    carry = step(i, carry, True)
    o_t = jnp.concatenate([acc[:DHB] / acc[DHB:DHB + 1] for _, acc in carry], axis=0)
    o_ref[...] = o_t.T.astype(BF16)


def _attn_prompt(qs, kaug, vt, bsz, S, tq):
    nq = S // tq
    once = pl.Buffered(1)
    return pl.pallas_call(
        functools.partial(_attn_kernel, tq=tq),
        grid=(bsz, nq),
        in_specs=[pl.BlockSpec((tq, DB), lambda b, i: (b * nq + i, 0)),
                  pl.BlockSpec((S, KAUG_WIDTH), lambda b, i: (b, 0), pipeline_mode=once),
                  pl.BlockSpec((nq, DB, tq), lambda b, i: (b, 0, 0), pipeline_mode=once)],
        out_specs=pl.BlockSpec((tq, DB), lambda b, i: (b * nq + i, 0)),
        out_shape=jax.ShapeDtypeStruct((bsz * S, DB), BF16),
        compiler_params=pltpu.CompilerParams(dimension_semantics=("parallel", "arbitrary"),
                                             vmem_limit_bytes=VMEM_LIMIT),
        name="attn_prompt",
    )(qs, kaug, vt)


def _attn_cached_kernel(q_ref, kc_ref, vc_ref, kn_ref, vn_ref, bhi_ref, bmid_ref, blo_ref, o_ref, *, L, P):
    q = q_ref[0]
    lane_head = lax.broadcasted_iota(jnp.int32, (L, DB), 1) // DHB
    q_bd = jnp.concatenate([jnp.where(lane_head == h, q, jnp.zeros_like(q)) for h in range(HB)], axis=0)
    kc = kc_ref[...].astype(BF16)
    vc = vc_ref[...].astype(BF16)
    zpad = jnp.zeros((LANES - L, DB), BF16)
    kn = jnp.concatenate([kn_ref[0].astype(BF16), zpad], axis=0)
    vn = jnp.concatenate([vn_ref[0].astype(BF16), zpad], axis=0)
    bias = (bhi_ref[0].astype(F32) + bmid_ref[0].astype(F32)) + blo_ref[0].astype(F32)

    s_c = _dot(q_bd, kc)
    s_n = _dot_nt(q_bd, kn)
    tt = lax.broadcasted_iota(jnp.int32, (L, LANES), 0)
    uu = lax.broadcasted_iota(jnp.int32, (L, LANES), 1)
    causal = uu <= tt
    pcs, pns, ls = [], [], []
    for h in range(HB):
        sc = s_c[h * L:(h + 1) * L, :] + bias[h:h + 1, :P]
        sn = jnp.where(causal, s_n[h * L:(h + 1) * L, :] + bias[h:h + 1, P:], NEG_BIG)
        m = jnp.maximum(jnp.max(sc, axis=1, keepdims=True), jnp.max(sn, axis=1, keepdims=True))
        pc = jnp.exp2(sc - m)
        pn = jnp.exp2(sn - m)
        ls.append(jnp.sum(pc, axis=1, keepdims=True) + jnp.sum(pn, axis=1, keepdims=True))
        pcs.append(pc.astype(BF16))
        pns.append(pn.astype(BF16))
    o_all = _dot_nt(jnp.concatenate(pcs, axis=0), vc) + _dot(jnp.concatenate(pns, axis=0), vn)
    out = jnp.zeros((L, DB), F32)
    for h in range(HB):
        out = out + jnp.where(lane_head == h, o_all[h * L:(h + 1) * L, :] / ls[h], 0.0)
    o_ref[0] = out.astype(BF16)


def _attn_cached(qs, k_cache_t, v_cache_t, layer, k_new, v_new, bhi, bmid, blo):
    bsz, L, _ = qs.shape
    P = k_cache_t.shape[-1]
    seq = lambda r, w: pl.BlockSpec((1, r, w), lambda b: (b, 0, 0))
    cache = pl.BlockSpec((None, None, DB, P), lambda b: (layer, b, 0, 0))
    return pl.pallas_call(
        functools.partial(_attn_cached_kernel, L=L, P=P),
        grid=(bsz,),
        in_specs=[seq(L, DB), cache, cache, seq(L, DB), seq(L, DB),
                  seq(HB, P + LANES), seq(HB, P + LANES), seq(HB, P + LANES)],
        out_specs=seq(L, DB),
        out_shape=jax.ShapeDtypeStruct((bsz, L, DB), BF16),
        compiler_params=pltpu.CompilerParams(dimension_semantics=("parallel",),
                                             vmem_limit_bytes=VMEM_LIMIT),
        name="attn_cached",
    )(qs, k_cache_t, v_cache_t, k_new, v_new, bhi, bmid, blo)


def _hgrn_exponent_matrix(C):
    nl = int(math.log2(C))
    rows = []
    idx = np.arange(C)
    for lv in range(nl):
        sz = 1 << lv
        m = np.zeros((C, C), np.float32)
        for t in range(C):
            bnd = ((t >> (lv + 1)) << (lv + 1)) + sz - 1
            if (t >> lv) & 1:
                m[t, (idx > bnd) & (idx <= t)] = 1.0
            else:
                m[t, (idx > t) & (idx <= bnd)] = 1.0
        rows.append(m)
    rows.append((idx[None, :] <= idx[:, None]).astype(np.float32))
    rows.append((idx[None, :] > idx[:, None]).astype(np.float32))
    p = np.concatenate(rows, axis=0)
    return np.concatenate([p, p], axis=1)


def _hgrn_kernel(p_ref, qh_ref, lf_ref, kk_ref, vv_ref, og_ref, s0_ref, g_ref, hg_ref, sout_ref, st_ref,
                 *, C, T):
    t = pl.program_id(1)
    nl = int(math.log2(C))

    @pl.when(t == 0)
    def _():
        for h in range(HC):
            st_ref[h] = s0_ref[0, h].T

    row = lax.broadcasted_iota(jnp.int32, (C, C), 0)
    col = lax.broadcasted_iota(jnp.int32, (C, C), 1)
    diff = row ^ col
    masks = [(jnp.right_shift(diff, lv) == 1) & ((jnp.right_shift(row, lv) & 1) == 1) for lv in range(nl)]
    diag = row == col
    g = g_ref[...]
    pmat = p_ref[...]

    pre = {}
    for c0 in range(0, T, C):
        lf = LOG2E * lf_ref[0, c0:c0 + C, :]
        hi = lf.astype(BF16)
        lo = (lf - hi.astype(F32)).astype(BF16)
        e_all = jnp.exp2(_dot(pmat, jnp.concatenate([hi, lo], axis=0)))
        for h in range(HC):
            hs = slice(h * DK, (h + 1) * DK)
            qb = qh_ref[0, c0:c0 + C, hs]
            kb = kk_ref[0, c0:c0 + C, hs]
            q = qb.astype(F32)
            k = kb.astype(F32)
            a = jnp.where(diag, _dot_nt(qb, kb), 0.0)
            for lv in range(nl):
                e = e_all[lv * C:(lv + 1) * C, hs]
                a = jnp.where(masks[lv], _dot_nt((q * e).astype(BF16), (k * e).astype(BF16)), a)
            e_q = e_all[nl * C:(nl + 1) * C, hs]
            e_k = e_all[(nl + 1) * C:(nl + 2) * C, hs]
            pre[c0, h] = (a.astype(BF16), (q * e_q).astype(BF16), (k * e_k).astype(BF16), e_q[C - 1:C, :])

    for h in range(HC):
        hs = slice(h * DK, (h + 1) * DK)
        st = st_ref[h]
        for c0 in range(0, T, C):
            a, q_dec, k_dec, chunk_dec = pre[c0, h]
            v = vv_ref[0, c0:c0 + C, hs]
            o = _dot(a, v) + _dot_nt(q_dec, st.astype(BF16))
            st = st * chunk_dec + _dot_tn(v, k_dec)
            o = o * lax.rsqrt(jnp.mean(o * o, axis=-1, keepdims=True) + EPS) * g
            hg_ref[0, c0:c0 + C, hs] = (o * og_ref[0, c0:c0 + C, hs].astype(F32)).astype(BF16)
        st_ref[h] = st

    @pl.when(t == pl.num_programs(1) - 1)
    def _():
        for h in range(HC):
            sout_ref[0, h] = st_ref[h].T


def _hgrn(qh, lfc, kk, vv, og, s0, g, layer, C, T):
    bsz, L, _ = qh.shape
    pmat = jnp.asarray(_hgrn_exponent_matrix(C), BF16)
    tile = pl.BlockSpec((1, T, DC), lambda b, t: (b, t, 0))
    state = pl.BlockSpec((1, HC, DK, DV), lambda b, t: (b, 0, 0, 0))
    return pl.pallas_call(
        functools.partial(_hgrn_kernel, C=C, T=T),
        grid=(bsz, L // T),
        in_specs=[_const_spec(pmat.shape), tile, tile, tile, tile, tile, state, _layer_spec((1, DV), layer)],
        out_specs=(tile, state),
        out_shape=(jax.ShapeDtypeStruct((bsz, L, DC), BF16),
                   jax.ShapeDtypeStruct((bsz, HC, DK, DV), F32)),
        scratch_shapes=[pltpu.VMEM((HC, DV, DK), F32)],
        compiler_params=pltpu.CompilerParams(dimension_semantics=("parallel", "arbitrary")),
        name="hgrn",
    )(pmat, qh, lfc, kk, vv, og, s0, g)


FF_CHUNK = 1024


def _merge_ffn_kernel(x_ref, fa_ref, fb_ref, fc_ref, g1_ref, wg_ref, wa_ref, wb_ref, wc_ref, wo_ref,
                      g2_ref, wup_ref, wdn_ref, gf_ref, o_ref, *, final):
    x = x_ref[...]
    h1 = _rms(x, g1_ref[...]).astype(BF16)
    m = None
    for i, (f_ref, w_ref) in enumerate(((fa_ref, wa_ref), (fb_ref, wb_ref), (fc_ref, wc_ref))):
        gate = _sigmoid(_dot(h1, wg_ref[:, i * D_MODEL:(i + 1) * D_MODEL]))
        y = gate * _dot(f_ref[...], w_ref[...])
        m = y if m is None else m + y
    x = x + _dot(m.astype(BF16), wo_ref[...])
    h2 = _rms(x, g2_ref[...]).astype(BF16)
    acc = None
    for c in range(0, D_FF, FF_CHUNK):
        up = jnp.maximum(_dot(h2, wup_ref[:, c:c + FF_CHUNK]), 0.0)
        d = _dot((up * up).astype(BF16), wdn_ref[c:c + FF_CHUNK, :])
        acc = d if acc is None else acc + d
    x = x + acc
    if final:
        x = _rms(x, gf_ref[...])
    o_ref[...] = x


def _merge_ffn(x, fa, fb, fc, w, layer, final_g, tm, final):
    n = x.shape[0]
    row = lambda w_: pl.BlockSpec((tm, w_), lambda i: (i, 0))
    lw = lambda *shape: _layer_spec(shape, layer)
    return pl.pallas_call(
        functools.partial(_merge_ffn_kernel, final=final),
        grid=(n // tm,),
        in_specs=[row(D_MODEL), row(CA), row(DB), row(DC),
                  lw(1, D_MODEL), lw(D_MODEL, 3 * D_MODEL),
                  lw(CA, D_MODEL), lw(DB, D_MODEL), lw(DC, D_MODEL),
                  lw(D_MODEL, D_MODEL), lw(1, D_MODEL),
                  lw(D_MODEL, D_FF), lw(D_FF, D_MODEL), _const_spec((1, D_MODEL))],
        out_specs=row(D_MODEL),
        out_shape=jax.ShapeDtypeStruct((n, D_MODEL), F32),
        compiler_params=pltpu.CompilerParams(dimension_semantics=("parallel",),
                                             vmem_limit_bytes=VMEM_LIMIT),
        name="merge_ffn",
    )(x, fa, fb, fc, w["g1"], w["w_gate"], w["w_a_out"], w["w_b_out"], w["w_c_out"], w["w_o"],
      w["g2"], w["w_up"], w["w_down"], final_g)


def _stacked_weights(p, lbs):
    w_in = p["w_in"]
    cols = lambda a, b: w_in[:, :, a:b]
    w_a = jnp.concatenate([cols(O_AV, O_QB), cols(O_QB, O_FB), cols(O_QC, O_GA),
                           jnp.pad(cols(O_FB, O_QC), ((0, 0), (0, 0), (0, LANES - HB)))], axis=2).astype(BF16)
    row = lambda v: v[:, None, :].astype(F32)
    lbrows = jnp.concatenate([jnp.log(lbs)[:, None], jnp.log1p(-lbs)[:, None], (1.0 - lbs)[:, None],
                              jnp.zeros((lbs.shape[0], 5, HC * DK), F32)], axis=1)
    return dict(
        w_a=w_a, g1=row(p["norm1_g"]),
        fbias=row(jnp.pad(p["fox_bf"], ((0, 0), (0, LANES - HB)))),
        lbrows=lbrows,
        conv_w=jnp.pad(p["conv_w"], ((0, 0), (0, HIST_ROWS - CONV_W), (0, 0))),
        conv_b=row(p["conv_b"]), ln_g=row(p["conv_ln_g"]), ln_b=row(p["conv_ln_b"]),
        hn_g=row(p["hgrn_norm_g"]),
        w_gate=cols(O_GA, O_END).astype(BF16),
        w_a_out=p["w_a_out"].astype(BF16), w_b_out=p["w_b_out"].astype(BF16),
        w_c_out=p["w_c_out"].astype(BF16), w_o=p["w_o"].astype(BF16),
        g2=row(p["norm2_g"]), w_up=p["w_up"].astype(BF16), w_down=p["w_down"].astype(BF16),
    )


def _pad_hist(h):
    return jnp.pad(h, ((0, 0), (HIST_PAD, 0), (0, 0)))


def _trunk_layer(x, w, layer, depth, stacks, final_g, final, conv_hist, s0, cache,
                 *, tm_a, tm_d, tl, tq, hg_tile):
    bsz, L, _ = x.shape
    n = bsz * L
    xf = x.reshape(n, D_MODEL)
    u, qs, k_all, kaug, v_all, vt, lf_all, qh, lfc, kk, vv, og = _in_proj(
        xf, w, layer, depth, stacks, tm_a, max(L // tm_a, 1))
    seq = lambda a: a.reshape(bsz, L, a.shape[-1])

    feat, conv_new = _conv(seq(u), _pad_hist(conv_hist), w, layer, tl)

    if cache is None:
        o_b = _attn_prompt(qs, kaug, vt, bsz, L, tq)
    else:
        k_cache_t, v_cache_t, lf_cache = cache
        P = k_cache_t.shape[-1]
        lf_t = seq(lf_all[layer]).transpose(0, 2, 1)
        lf_cat = jnp.concatenate([lf_cache.transpose(0, 2, 1), lf_t,
                                  jnp.zeros((bsz, HB, LANES - L), F32)], axis=-1)
        hi, mid, lo = _cumsum_bias(lf_cat, pivot=P - 1)
        o_b = _attn_cached(seq(qs), k_cache_t, v_cache_t, layer,
                           seq(k_all[layer]), seq(v_all[layer]), hi, mid, lo).reshape(n, DB)

    C = min(HGRN_CHUNK, L)
    hg, s_new = _hgrn(seq(qh), seq(lfc), seq(kk), seq(vv), seq(og), s0, w["hn_g"], layer, C, hg_tile)

    x_new = _merge_ffn(xf, feat.reshape(n, CA), o_b, hg.reshape(n, DC), w, layer, final_g, tm_d, final)
    return x_new.reshape(bsz, L, D_MODEL), (k_all, v_all, lf_all), conv_new[:, HIST_PAD:], s_new


def _lower_bounds(p):
    s = jax.nn.softmax(p.astype(F32), axis=0)
    return jnp.maximum(jnp.cumsum(s, axis=0) - s[0], 0.0)


def kernel(x_prompt, x_sample, cache_fox_k, cache_fox_v, cache_fox_logf, state_conv, state_hgrn,
           norm1_g, w_in, conv_w, conv_b, conv_ln_g, conv_ln_b, w_a_out, fox_bf, w_b_out,
           hgrn_lb_param, hgrn_norm_g, w_c_out, w_o, norm2_g, w_up, w_down, final_g):
    p = dict(norm1_g=norm1_g, w_in=w_in, conv_w=conv_w, conv_b=conv_b, conv_ln_g=conv_ln_g,
             conv_ln_b=conv_ln_b, w_a_out=w_a_out, fox_bf=fox_bf, w_b_out=w_b_out,
             hgrn_norm_g=hgrn_norm_g, w_c_out=w_c_out, w_o=w_o, norm2_g=norm2_g, w_up=w_up, w_down=w_down)
    depth = w_in.shape[0]
    lbs = _lower_bounds(hgrn_lb_param)
    fg = final_g.reshape(1, D_MODEL).astype(F32)
    xp, xs = x_prompt, x_sample
    bp, sp, _ = xp.shape
    bs, ls, _ = xs.shape
    zero_hist = jnp.zeros((bp, CONV_W - 1, CA), F32)
    zero_s = jnp.zeros((bp, HC, DK, DV), F32)
    past = cache_fox_k.shape[2]
    k_cache_t = cache_fox_k.transpose(0, 1, 3, 4, 2).reshape(depth, bs, DB, past)
    v_cache_t = cache_fox_v.transpose(0, 1, 3, 4, 2).reshape(depth, bs, DB, past)
    w = _stacked_weights(p, lbs)
    stacks_p = stacks_s = None
    conv_p, conv_s, hgrn_p, hgrn_s = [], [], [], []
    for l in range(depth):
        final = l == depth - 1
        xp, stacks_p, cp, hp = _trunk_layer(
            xp, w, l, depth, stacks_p, fg, final, zero_hist, zero_s, None,
            tm_a=min(512, bp * sp), tm_d=min(256, bp * sp), tl=min(512, sp),
            tq=min(512, sp), hg_tile=min(512, sp))
        xs, stacks_s, cs, hs = _trunk_layer(
            xs, w, l, depth, stacks_s, fg, final, state_conv[l], state_hgrn[l],
            (k_cache_t, v_cache_t, cache_fox_logf[l]),
            tm_a=min(512, bs * ls), tm_d=min(256, bs * ls), tl=ls, tq=None, hg_tile=ls)
        conv_p.append(cp); conv_s.append(cs); hgrn_p.append(hp); hgrn_s.append(hs)
    kp, vp, lfp = stacks_p
    ks, vs, lfs = stacks_s
    return (xp, xs,
            kp.reshape(depth, bp, sp, HB, DHB), vp.reshape(depth, bp, sp, HB, DHB),
            lfp.reshape(depth, bp, sp, HB), jnp.stack(conv_p), jnp.stack(hgrn_p),
            ks.reshape(depth, bs, ls, HB, DHB), vs.reshape(depth, bs, ls, HB, DHB),
            lfs.reshape(depth, bs, ls, HB), jnp.stack(conv_s), jnp.stack(hgrn_s))
```

```python
import functools
import math

import numpy as np
import jax
import jax.numpy as jnp
from jax import lax
from jax.experimental import pallas as pl
from jax.experimental.pallas import tpu as pltpu

D_MODEL = 1024
CONV_W = 31
CA = 512
HB = 8
DHB = 64
DB = HB * DHB
HC = 4
DK = 128
DV = 128
DC = HC * DV
D_FF = 4 * D_MODEL
EPS = 1e-6
HGRN_CHUNK = 128

LANES = 128
SUBLANES = 8
VMEM_LIMIT = 56 * 1024 * 1024
NEG_BIG = -1e30
LOG2E = 1.4426950408889634

F32 = jnp.float32
BF16 = jnp.bfloat16

_OFF = np.cumsum([0, CA, CA, DB, DB, DB, HB, HC * DK, HC * DK, DC, DC, D_MODEL, D_MODEL, D_MODEL])
(O_AV, O_AG, O_QB, O_KB, O_VB, O_FB, O_QC, O_FC, O_IC, O_OC, O_GA, O_GB, O_GC, O_END) = [int(v) for v in _OFF]


def _dot(a, b):
    return jnp.dot(a, b, preferred_element_type=F32)


def _dot_nt(a, b):
    return lax.dot_general(a, b, (((1,), (1,)), ((), ())), preferred_element_type=F32)


def _dot_tn(a, b):
    return lax.dot_general(a, b, (((0,), (0,)), ((), ())), preferred_element_type=F32)


def _sigmoid(x):
    return 1.0 / (1.0 + jnp.exp(-x))


def _silu(x):
    return x * _sigmoid(x)


def _rms(x, g):
    return x * lax.rsqrt(jnp.mean(x * x, axis=-1, keepdims=True) + EPS) * g


def _split3(x):
    hi = x.astype(BF16)
    r = x - hi.astype(F32)
    mid = r.astype(BF16)
    lo = (r - mid.astype(F32)).astype(BF16)
    return hi, mid, lo


def _const_spec(shape):
    nd = len(shape)
    return pl.BlockSpec(shape, lambda *_: (0,) * nd, pipeline_mode=pl.Buffered(1))


def _layer_spec(shape, layer):
    nd = len(shape)
    return pl.BlockSpec((None,) + tuple(shape), lambda *_: (layer,) + (0,) * nd, pipeline_mode=pl.Buffered(1))


A_WIDTH = 9 * 512 + LANES
PAIR_LANES = 2 * LANES
KAUG_WIDTH = (HB // 2) * PAIR_LANES


def _bias_placement_matrix():
    m = np.zeros((3 * LANES, (HB // 2) * LANES), np.float32)
    for head in range(HB):
        pair, e = divmod(head, 2)
        for part in range(3):
            m[part * LANES + head, pair * LANES + 3 * e + part] = 1.0
    return m


def _in_proj_kernel(x_ref, g1_ref, w_ref, fbias_ref, lb_ref, tri_ref, place_ref, *rest,
                    tiles_per_seq, n_alias, feature_major):
    (u_ref, qs_ref, k_ref, kaug_ref, v_ref, vt_ref, lf_ref,
     qh_ref, lfc_ref, kk_ref, vv_ref, og_ref, carry_ref) = rest[n_alias:]
    i = pl.program_id(0)

    @pl.when(i % tiles_per_seq == 0)
    def _():
        carry_ref[...] = jnp.zeros_like(carry_ref)

    x = x_ref[...]
    h = _rms(x, g1_ref[...]).astype(BF16)

    def seg(i, width=512):
        return _dot(h, w_ref[:, i * 512:i * 512 + width])

    f = seg(9, LANES) + fbias_ref[...]
    lf = jnp.minimum(f, 0.0) - jnp.log(1.0 + jnp.exp(-jnp.abs(f)))
    lf_ref[...] = lf[:, :HB]

    hi, mid, lo = _split3(lf)
    c3 = _dot(tri_ref[...], jnp.concatenate([hi, mid, lo], axis=1))
    c = (c3[:, :LANES] + c3[:, LANES:2 * LANES]) + c3[:, 2 * LANES:] + carry_ref[0:1, :]
    carry_ref[0:1, :] = c[c.shape[0] - 1:, :]
    bh, bm, bl = _split3(-LOG2E * c)
    placed = _dot(jnp.concatenate([bh, bm, bl], axis=1), place_ref[...]).astype(BF16)

    u_ref[...] = seg(0) * _sigmoid(seg(1))
    qs_ref[...] = (seg(2) * (LOG2E * DHB ** -0.5)).astype(BF16)
    k = seg(3)
    for p in range(HB // 2):
        kaug_ref[:, 2 * p * LANES:(2 * p + 1) * LANES] = k[:, p * LANES:(p + 1) * LANES].astype(BF16)
        kaug_ref[:, (2 * p + 1) * LANES:(2 * p + 2) * LANES] = placed[:, p * LANES:(p + 1) * LANES]
    v = seg(4)
    v_t = v.T
    vt_ref[...] = v_t.astype(BF16)
    k_ref[...] = k.T if feature_major else k
    v_ref[...] = v_t if feature_major else v
    qh_ref[...] = _silu(seg(5)).astype(BF16)

    z = seg(6)
    log_lb = lb_ref[0:1, :]
    log1m_lb = lb_ref[1:2, :]
    one_m_lb = lb_ref[2:3, :]
    e = jnp.exp(-jnp.abs(z))
    ls = jnp.minimum(z, 0.0) - jnp.log(1.0 + e)
    b = log1m_lb + ls
    mx = jnp.maximum(log_lb, b)
    lfc_ref[...] = mx + jnp.log(1.0 + jnp.exp(-jnp.abs(log_lb - b)))
    r = 1.0 / (1.0 + e)
    kk_ref[...] = (one_m_lb * jnp.where(z >= 0.0, e * r, r)).astype(BF16)

    vv_ref[...] = seg(7).astype(BF16)
    og_ref[...] = _silu(seg(8)).astype(BF16)


def _in_proj(x, w, layer, depth, stacks, tm, tiles_per_seq, feature_major):
    n = x.shape[0]
    tok = lambda w_, dt: jax.ShapeDtypeStruct((n, w_), dt)
    row = lambda w_: pl.BlockSpec((tm, w_), lambda i: (i, 0))
    slab = lambda w_: pl.BlockSpec((None, tm, w_), lambda i: (layer, i, 0))
    stk = lambda w_: jax.ShapeDtypeStruct((depth, n, w_), F32)
    if feature_major:
        seq_len = tiles_per_seq * tm
        kv_stk = jax.ShapeDtypeStruct((depth, n // seq_len, DB, seq_len), F32)
        kv_slab = pl.BlockSpec((None, None, DB, tm),
                               lambda i: (layer, i // tiles_per_seq, 0, i % tiles_per_seq))
    else:
        kv_stk, kv_slab = stk(512), slab(512)
    tri = jnp.asarray(np.tril(np.ones((tm, tm), np.float32)), BF16)
    place = jnp.asarray(_bias_placement_matrix(), BF16)
    out_shape = (tok(512, F32), tok(512, BF16), kv_stk, tok(KAUG_WIDTH, BF16), kv_stk,
                 jax.ShapeDtypeStruct((n // tm, DB, tm), BF16), stk(HB),
                 tok(512, BF16), tok(512, F32), tok(512, BF16), tok(512, BF16), tok(512, BF16))
    out_specs = (row(512), row(512), kv_slab, row(KAUG_WIDTH), kv_slab,
                 pl.BlockSpec((None, DB, tm), lambda i: (i, 0, 0)), slab(HB),
                 row(512), row(512), row(512), row(512), row(512))
    in_specs = [row(D_MODEL), _layer_spec((1, D_MODEL), layer), _layer_spec((D_MODEL, A_WIDTH), layer),
                _layer_spec((1, LANES), layer), _layer_spec((8, 512), layer), _const_spec((tm, tm)),
                _const_spec(place.shape)]
    args = [x, w["g1"], w["w_a"], w["fbias"], w["lbrows"], tri, place]
    aliases = {}
    if stacks is not None:
        aliases = {len(args) + a: out_idx for a, out_idx in enumerate((2, 4, 6))}
        in_specs += [pl.BlockSpec(memory_space=pl.ANY)] * len(stacks)
        args += list(stacks)
    return pl.pallas_call(
        functools.partial(_in_proj_kernel, tiles_per_seq=tiles_per_seq, n_alias=len(aliases),
                          feature_major=feature_major),
        grid=(n // tm,),
        in_specs=in_specs,
        out_specs=out_specs,
        out_shape=out_shape,
        input_output_aliases=aliases,
        scratch_shapes=[pltpu.VMEM((8, LANES), F32)],
        compiler_params=pltpu.CompilerParams(dimension_semantics=("arbitrary",),
                                             vmem_limit_bytes=VMEM_LIMIT),
        name="in_proj",
    )(*args)


def _cumsum_kernel(lf_ref, hi_ref, mid_ref, lo_ref, *, n, pivot):
    x = lf_ref[...]
    lane = lax.broadcasted_iota(jnp.int32, x.shape, 1)
    s = 1
    while s < n:
        x = x + jnp.where(lane >= s, pltpu.roll(x, s, 1), 0.0)
        s *= 2
    piv = jnp.sum(jnp.where(lane == pivot, x, 0.0), axis=1, keepdims=True)
    hi, mid, lo = _split3(LOG2E * (piv - x))
    hi_ref[...] = hi
    mid_ref[...] = mid
    lo_ref[...] = lo


def _cumsum_bias(lf_t, pivot):
    bsz, hh, n = lf_t.shape
    spec = pl.BlockSpec((bsz * hh, n), lambda i: (0, 0))
    sds = jax.ShapeDtypeStruct((bsz * hh, n), BF16)
    outs = pl.pallas_call(
        functools.partial(_cumsum_kernel, n=n, pivot=pivot),
        grid=(1,),
        in_specs=[spec],
        out_specs=(spec, spec, spec),
        out_shape=(sds, sds, sds),
        name="cumsum_bias",
    )(lf_t.reshape(bsz * hh, n))
    return tuple(o.reshape(bsz, hh, n) for o in outs)


HIST_ROWS = 32
HIST_PAD = HIST_ROWS - (CONV_W - 1)


def _conv_kernel(u_ref, hist_ref, w_ref, cb_ref, g_ref, b_ref, feat_ref, new_ref, buf_ref, *, tl, rc):
    t = pl.program_id(1)

    @pl.when(t == 0)
    def _():
        buf_ref[0:HIST_ROWS, :] = hist_ref[0]

    @pl.when(t > 0)
    def _():
        buf_ref[0:HIST_ROWS, :] = buf_ref[tl:tl + HIST_ROWS, :]

    buf_ref[HIST_ROWS:HIST_ROWS + tl, :] = u_ref[0]
    new_ref[0] = buf_ref[tl:tl + HIST_ROWS, :]

    cb = cb_ref[...]
    g = g_ref[...]
    b = b_ref[...]
    for r0 in range(0, tl, rc):
        y = cb
        for r in range(SUBLANES):
            z = None
            for a in range((HIST_ROWS + SUBLANES) // SUBLANES):
                j = SUBLANES * a + r - HIST_PAD
                if 0 <= j < CONV_W:
                    nrows = rc + (SUBLANES if r else 0)
                    term = w_ref[j:j + 1, :] * buf_ref[r0 + SUBLANES * a:r0 + SUBLANES * a + nrows, :]
                    z = term if z is None else z + term
            y = y + z[r:r + rc]
        mu = jnp.mean(y, axis=-1, keepdims=True)
        yc = y - mu
        yn = yc * lax.rsqrt(jnp.mean(yc * yc, axis=-1, keepdims=True) + EPS) * g + b
        feat_ref[0, r0:r0 + rc, :] = _silu(yn).astype(BF16)


def _conv(u, hist, w, layer, tl):
    bsz, L, _ = u.shape
    rc = min(tl, 64)
    vec = _layer_spec((1, CA), layer)
    return pl.pallas_call(
        functools.partial(_conv_kernel, tl=tl, rc=rc),
        grid=(bsz, L // tl),
        in_specs=[pl.BlockSpec((1, tl, CA), lambda i, t: (i, t, 0)),
                  pl.BlockSpec((1, HIST_ROWS, CA), lambda i, t: (i, 0, 0)),
                  _layer_spec((HIST_ROWS, CA), layer), vec, vec, vec],
        out_specs=(pl.BlockSpec((1, tl, CA), lambda i, t: (i, t, 0)),
                   pl.BlockSpec((1, HIST_ROWS, CA), lambda i, t: (i, 0, 0))),
        out_shape=(jax.ShapeDtypeStruct((bsz, L, CA), BF16),
                   jax.ShapeDtypeStruct((bsz, HIST_ROWS, CA), F32)),
        scratch_shapes=[pltpu.VMEM((tl + HIST_ROWS, CA), F32)],
        compiler_params=pltpu.CompilerParams(dimension_semantics=("parallel", "arbitrary")),
        name="conv",
    )(u, hist, w["conv_w"], w["conv_b"], w["ln_g"], w["ln_b"])


ONES_ROWS = 16


def _attn_kernel(q_ref, k_ref, vt_ref, o_ref, *, tq):
    i = pl.program_id(1)
    lane = lax.broadcasted_iota(jnp.int32, (tq, LANES), 1)
    qm = []
    for p in range(HB // 2):
        q = q_ref[:, p * LANES:(p + 1) * LANES]
        for e in range(2):
            qa = jnp.where(lane // DHB == e, q, jnp.zeros_like(q))
            pick = jnp.where((lane >= 3 * e) & (lane < 3 * e + 3), 1.0, 0.0).astype(BF16)
            qm.append(jnp.concatenate([qa, pick], axis=1))
    ones = jnp.ones((ONES_ROWS, tq), BF16)

    def step(j, carry, diag):
        rows = pl.ds(pl.multiple_of(j * tq, tq), tq)
        if diag:
            kk = lax.broadcasted_iota(jnp.int32, (tq, tq), 0)
            qq = lax.broadcasted_iota(jnp.int32, (tq, tq), 1)
            keep = kk <= qq
        ss = []
        for p in range(HB // 2):
            kblk = k_ref[rows, p * PAIR_LANES:(p + 1) * PAIR_LANES]
            for e in range(2):
                ss.append(_dot_nt(kblk, qm[2 * p + e]))
        if diag:
            ss = [jnp.where(keep, s, NEG_BIG) for s in ss]
        ms = [jnp.maximum(carry[h][0], jnp.max(ss[h], axis=0, keepdims=True)) for h in range(HB)]
        out = []
        for h in range(HB):
            m, acc = carry[h]
            alpha = jnp.exp2(m - ms[h])
            p = jnp.exp2(ss[h] - ms[h]).astype(BF16)
            vt = jnp.concatenate([vt_ref[j, h * DHB:(h + 1) * DHB, :], ones], axis=0)
            out.append((ms[h], acc * alpha + _dot(vt, p)))
        return tuple(out)

    init = tuple((jnp.full((1, tq), NEG_BIG, F32), jnp.zeros((DHB + ONES_ROWS, tq), F32))
                 for _ in range(HB))
    carry = lax.fori_loop(0, i, lambda j, c: step(j, c, False), init)
    carry = step(i, carry, True)
    o_t = jnp.concatenate([acc[:DHB] / acc[DHB:DHB + 1] for _, acc in carry], axis=0)
    o_ref[...] = o_t.T.astype(BF16)


def _attn_prompt(qs, kaug, vt, bsz, S, tq):
    nq = S // tq
    once = pl.Buffered(1)
    return pl.pallas_call(
        functools.partial(_attn_kernel, tq=tq),
        grid=(bsz, nq),
        in_specs=[pl.BlockSpec((tq, DB), lambda b, i: (b * nq + i, 0)),
                  pl.BlockSpec((S, KAUG_WIDTH), lambda b, i: (b, 0), pipeline_mode=once),
                  pl.BlockSpec((nq, DB, tq), lambda b, i: (b, 0, 0), pipeline_mode=once)],
        out_specs=pl.BlockSpec((tq, DB), lambda b, i: (b * nq + i, 0)),
        out_shape=jax.ShapeDtypeStruct((bsz * S, DB), BF16),
        compiler_params=pltpu.CompilerParams(dimension_semantics=("parallel", "arbitrary"),
                                             vmem_limit_bytes=VMEM_LIMIT),
        name="attn_prompt",
    )(qs, kaug, vt)


def _attn_cached_kernel(q_ref, kc_ref, vc_ref, kn_ref, vn_ref, bhi_ref, bmid_ref, blo_ref, o_ref, *, L, P):
    q = q_ref[0]
    lane_head = lax.broadcasted_iota(jnp.int32, (L, DB), 1) // DHB
    q_bd = jnp.concatenate([jnp.where(lane_head == h, q, jnp.zeros_like(q)) for h in range(HB)], axis=0)
    kc = kc_ref[...].astype(BF16)
    vc = vc_ref[...].astype(BF16)
    zpad = jnp.zeros((LANES - L, DB), BF16)
    kn = jnp.concatenate([kn_ref[0].astype(BF16), zpad], axis=0)
    vn = jnp.concatenate([vn_ref[0].astype(BF16), zpad], axis=0)
    bias = (bhi_ref[0].astype(F32) + bmid_ref[0].astype(F32)) + blo_ref[0].astype(F32)

    s_c = _dot(q_bd, kc)
    s_n = _dot_nt(q_bd, kn)
    tt = lax.broadcasted_iota(jnp.int32, (L, LANES), 0)
    uu = lax.broadcasted_iota(jnp.int32, (L, LANES), 1)
    causal = uu <= tt
    pcs, pns, ls = [], [], []
    for h in range(HB):
        sc = s_c[h * L:(h + 1) * L, :] + bias[h:h + 1, :P]
        sn = jnp.where(causal, s_n[h * L:(h + 1) * L, :] + bias[h:h + 1, P:], NEG_BIG)
        m = jnp.maximum(jnp.max(sc, axis=1, keepdims=True), jnp.max(sn, axis=1, keepdims=True))
        pc = jnp.exp2(sc - m)
        pn = jnp.exp2(sn - m)
        ls.append(jnp.sum(pc, axis=1, keepdims=True) + jnp.sum(pn, axis=1, keepdims=True))
        pcs.append(pc.astype(BF16))
        pns.append(pn.astype(BF16))
    o_all = _dot_nt(jnp.concatenate(pcs, axis=0), vc) + _dot(jnp.concatenate(pns, axis=0), vn)
    out = jnp.zeros((L, DB), F32)
    for h in range(HB):
        out = out + jnp.where(lane_head == h, o_all[h * L:(h + 1) * L, :] / ls[h], 0.0)
    o_ref[0] = out.astype(BF16)


def _attn_cached(qs, k_cache_t, v_cache_t, layer, k_new, v_new, bhi, bmid, blo):
    bsz, L, _ = qs.shape
    P = k_cache_t.shape[-1]
    seq = lambda r, w: pl.BlockSpec((1, r, w), lambda b: (b, 0, 0))
    cache = pl.BlockSpec((None, None, DB, P), lambda b: (layer, b, 0, 0))
    return pl.pallas_call(
        functools.partial(_attn_cached_kernel, L=L, P=P),
        grid=(bsz,),
        in_specs=[seq(L, DB), cache, cache, seq(L, DB), seq(L, DB),
                  seq(HB, P + LANES), seq(HB, P + LANES), seq(HB, P + LANES)],
        out_specs=seq(L, DB),
        out_shape=jax.ShapeDtypeStruct((bsz, L, DB), BF16),
        compiler_params=pltpu.CompilerParams(dimension_semantics=("parallel",),
                                             vmem_limit_bytes=VMEM_LIMIT),
        name="attn_cached",
    )(qs, k_cache_t, v_cache_t, k_new, v_new, bhi, bmid, blo)


def _hgrn_exponent_matrix(C):
    nl = int(math.log2(C))
    rows = []
    idx = np.arange(C)
    for lv in range(nl):
        sz = 1 << lv
        m = np.zeros((C, C), np.float32)
        for t in range(C):
            bnd = ((t >> (lv + 1)) << (lv + 1)) + sz - 1
            if (t >> lv) & 1:
                m[t, (idx > bnd) & (idx <= t)] = 1.0
            else:
                m[t, (idx > t) & (idx <= bnd)] = 1.0
        rows.append(m)
    rows.append((idx[None, :] <= idx[:, None]).astype(np.float32))
    rows.append((idx[None, :] > idx[:, None]).astype(np.float32))
    p = np.concatenate(rows, axis=0)
    return np.concatenate([p, p], axis=1)


def _hgrn_kernel(p_ref, qh_ref, lf_ref, kk_ref, vv_ref, og_ref, s0_ref, g_ref, hg_ref, sout_ref, st_ref,
                 *, C, T):
    t = pl.program_id(1)
    nl = int(math.log2(C))

    @pl.when(t == 0)
    def _():
        for h in range(HC):
            st_ref[h] = s0_ref[0, h].T

    row = lax.broadcasted_iota(jnp.int32, (C, C), 0)
    col = lax.broadcasted_iota(jnp.int32, (C, C), 1)
    diff = row ^ col
    masks = [(jnp.right_shift(diff, lv) == 1) & ((jnp.right_shift(row, lv) & 1) == 1) for lv in range(nl)]
    diag = row == col
    g = g_ref[...]
    pmat = p_ref[...]

    pre = {}
    for c0 in range(0, T, C):
        lf = LOG2E * lf_ref[0, c0:c0 + C, :]
        hi = lf.astype(BF16)
        lo = (lf - hi.astype(F32)).astype(BF16)
        e_all = jnp.exp2(_dot(pmat, jnp.concatenate([hi, lo], axis=0)))
        for h in range(HC):
            hs = slice(h * DK, (h + 1) * DK)
            qb = qh_ref[0, c0:c0 + C, hs]
            kb = kk_ref[0, c0:c0 + C, hs]
            q = qb.astype(F32)
            k = kb.astype(F32)
            a = jnp.where(diag, _dot_nt(qb, kb), 0.0)
            for lv in range(nl):
                e = e_all[lv * C:(lv + 1) * C, hs]
                a = jnp.where(masks[lv], _dot_nt((q * e).astype(BF16), (k * e).astype(BF16)), a)
            e_q = e_all[nl * C:(nl + 1) * C, hs]
            e_k = e_all[(nl + 1) * C:(nl + 2) * C, hs]
            pre[c0, h] = (a.astype(BF16), (q * e_q).astype(BF16), (k * e_k).astype(BF16), e_q[C - 1:C, :])

    for h in range(HC):
        hs = slice(h * DK, (h + 1) * DK)
        st = st_ref[h]
        for c0 in range(0, T, C):
            a, q_dec, k_dec, chunk_dec = pre[c0, h]
            v = vv_ref[0, c0:c0 + C, hs]
            o = _dot(a, v) + _dot_nt(q_dec, st.astype(BF16))
            st = st * chunk_dec + _dot_tn(v, k_dec)
            o = o * lax.rsqrt(jnp.mean(o * o, axis=-1, keepdims=True) + EPS) * g
            hg_ref[0, c0:c0 + C, hs] = (o * og_ref[0, c0:c0 + C, hs].astype(F32)).astype(BF16)
        st_ref[h] = st

    @pl.when(t == pl.num_programs(1) - 1)
    def _():
        for h in range(HC):
            sout_ref[0, h] = st_ref[h].T


def _hgrn(qh, lfc, kk, vv, og, s0, g, layer, C, T):
    bsz, L, _ = qh.shape
    pmat = jnp.asarray(_hgrn_exponent_matrix(C), BF16)
    tile = pl.BlockSpec((1, T, DC), lambda b, t: (b, t, 0))
    state = pl.BlockSpec((1, HC, DK, DV), lambda b, t: (b, 0, 0, 0))
    return pl.pallas_call(
        functools.partial(_hgrn_kernel, C=C, T=T),
        grid=(bsz, L // T),
        in_specs=[_const_spec(pmat.shape), tile, tile, tile, tile, tile, state, _layer_spec((1, DV), layer)],
        out_specs=(tile, state),
        out_shape=(jax.ShapeDtypeStruct((bsz, L, DC), BF16),
                   jax.ShapeDtypeStruct((bsz, HC, DK, DV), F32)),
        scratch_shapes=[pltpu.VMEM((HC, DV, DK), F32)],
        compiler_params=pltpu.CompilerParams(dimension_semantics=("parallel", "arbitrary")),
        name="hgrn",
    )(pmat, qh, lfc, kk, vv, og, s0, g)


FF_CHUNK = 1024


def _merge_ffn_kernel(x_ref, fa_ref, fb_ref, fc_ref, g1_ref, wg_ref, wa_ref, wb_ref, wc_ref, wo_ref,
                      g2_ref, wup_ref, wdn_ref, gf_ref, o_ref, *, final):
    x = x_ref[...]
    h1 = _rms(x, g1_ref[...]).astype(BF16)
    m = None
    for i, (f_ref, w_ref) in enumerate(((fa_ref, wa_ref), (fb_ref, wb_ref), (fc_ref, wc_ref))):
        gate = _sigmoid(_dot(h1, wg_ref[:, i * D_MODEL:(i + 1) * D_MODEL]))
        y = gate * _dot(f_ref[...], w_ref[...])
        m = y if m is None else m + y
    x = x + _dot(m.astype(BF16), wo_ref[...])
    h2 = _rms(x, g2_ref[...]).astype(BF16)
    acc = None
    for c in range(0, D_FF, FF_CHUNK):
        up = jnp.maximum(_dot(h2, wup_ref[:, c:c + FF_CHUNK]), 0.0)
        d = _dot((up * up).astype(BF16), wdn_ref[c:c + FF_CHUNK, :])
        acc = d if acc is None else acc + d
    x = x + acc
    if final:
        x = _rms(x, gf_ref[...])
    o_ref[...] = x


def _merge_ffn(x, fa, fb, fc, w, layer, final_g, tm, final):
    n = x.shape[0]
    row = lambda w_: pl.BlockSpec((tm, w_), lambda i: (i, 0))
    lw = lambda *shape: _layer_spec(shape, layer)
    return pl.pallas_call(
        functools.partial(_merge_ffn_kernel, final=final),
        grid=(n // tm,),
        in_specs=[row(D_MODEL), row(CA), row(DB), row(DC),
                  lw(1, D_MODEL), lw(D_MODEL, 3 * D_MODEL),
                  lw(CA, D_MODEL), lw(DB, D_MODEL), lw(DC, D_MODEL),
                  lw(D_MODEL, D_MODEL), lw(1, D_MODEL),
                  lw(D_MODEL, D_FF), lw(D_FF, D_MODEL), _const_spec((1, D_MODEL))],
        out_specs=row(D_MODEL),
        out_shape=jax.ShapeDtypeStruct((n, D_MODEL), F32),
        compiler_params=pltpu.CompilerParams(dimension_semantics=("parallel",),
                                             vmem_limit_bytes=VMEM_LIMIT),
        name="merge_ffn",
    )(x, fa, fb, fc, w["g1"], w["w_gate"], w["w_a_out"], w["w_b_out"], w["w_c_out"], w["w_o"],
      w["g2"], w["w_up"], w["w_down"], final_g)


def _stacked_weights(p, lbs):
    w_in = p["w_in"]
    cols = lambda a, b: w_in[:, :, a:b]
    w_a = jnp.concatenate([cols(O_AV, O_QB), cols(O_QB, O_FB), cols(O_QC, O_GA),
                           jnp.pad(cols(O_FB, O_QC), ((0, 0), (0, 0), (0, LANES - HB)))], axis=2).astype(BF16)
    row = lambda v: v[:, None, :].astype(F32)
    lbrows = jnp.concatenate([jnp.log(lbs)[:, None], jnp.log1p(-lbs)[:, None], (1.0 - lbs)[:, None],
                              jnp.zeros((lbs.shape[0], 5, HC * DK), F32)], axis=1)
    return dict(
        w_a=w_a, g1=row(p["norm1_g"]),
        fbias=row(jnp.pad(p["fox_bf"], ((0, 0), (0, LANES - HB)))),
        lbrows=lbrows,
        conv_w=jnp.pad(p["conv_w"], ((0, 0), (0, HIST_ROWS - CONV_W), (0, 0))),
        conv_b=row(p["conv_b"]), ln_g=row(p["conv_ln_g"]), ln_b=row(p["conv_ln_b"]),
        hn_g=row(p["hgrn_norm_g"]),
        w_gate=cols(O_GA, O_END).astype(BF16),
        w_a_out=p["w_a_out"].astype(BF16), w_b_out=p["w_b_out"].astype(BF16),
        w_c_out=p["w_c_out"].astype(BF16), w_o=p["w_o"].astype(BF16),
        g2=row(p["norm2_g"]), w_up=p["w_up"].astype(BF16), w_down=p["w_down"].astype(BF16),
    )


def _pad_hist(h):
    return jnp.pad(h, ((0, 0), (HIST_PAD, 0), (0, 0)))


def _trunk_layer(x, w, layer, depth, stacks, final_g, final, conv_hist, s0, cache,
                 *, tm_a, tm_d, tl, tq, hg_tile):
    bsz, L, _ = x.shape
    n = bsz * L
    xf = x.reshape(n, D_MODEL)
    u, qs, k_all, kaug, v_all, vt, lf_all, qh, lfc, kk, vv, og = _in_proj(
        xf, w, layer, depth, stacks, tm_a, max(L // tm_a, 1), feature_major=cache is None)
    seq = lambda a: a.reshape(bsz, L, a.shape[-1])

    feat, conv_new = _conv(seq(u), _pad_hist(conv_hist), w, layer, tl)

    if cache is None:
        o_b = _attn_prompt(qs, kaug, vt, bsz, L, tq)
    else:
        k_cache_t, v_cache_t, lf_cache = cache
        P = k_cache_t.shape[-1]
        lf_t = seq(lf_all[layer]).transpose(0, 2, 1)
        lf_cat = jnp.concatenate([lf_cache.transpose(0, 2, 1), lf_t,
                                  jnp.zeros((bsz, HB, LANES - L), F32)], axis=-1)
        hi, mid, lo = _cumsum_bias(lf_cat, pivot=P - 1)
        o_b = _attn_cached(seq(qs), k_cache_t, v_cache_t, layer,
                           seq(k_all[layer]), seq(v_all[layer]), hi, mid, lo).reshape(n, DB)

    C = min(HGRN_CHUNK, L)
    hg, s_new = _hgrn(seq(qh), seq(lfc), seq(kk), seq(vv), seq(og), s0, w["hn_g"], layer, C, hg_tile)

    x_new = _merge_ffn(xf, feat.reshape(n, CA), o_b, hg.reshape(n, DC), w, layer, final_g, tm_d, final)
    return x_new.reshape(bsz, L, D_MODEL), (k_all, v_all, lf_all), conv_new[:, HIST_PAD:], s_new


def _lower_bounds(p):
    s = jax.nn.softmax(p.astype(F32), axis=0)
    return jnp.maximum(jnp.cumsum(s, axis=0) - s[0], 0.0)


def kernel(x_prompt, x_sample, cache_fox_k, cache_fox_v, cache_fox_logf, state_conv, state_hgrn,
           norm1_g, w_in, conv_w, conv_b, conv_ln_g, conv_ln_b, w_a_out, fox_bf, w_b_out,
           hgrn_lb_param, hgrn_norm_g, w_c_out, w_o, norm2_g, w_up, w_down, final_g):
    p = dict(norm1_g=norm1_g, w_in=w_in, conv_w=conv_w, conv_b=conv_b, conv_ln_g=conv_ln_g,
             conv_ln_b=conv_ln_b, w_a_out=w_a_out, fox_bf=fox_bf, w_b_out=w_b_out,
             hgrn_norm_g=hgrn_norm_g, w_c_out=w_c_out, w_o=w_o, norm2_g=norm2_g, w_up=w_up, w_down=w_down)
    depth = w_in.shape[0]
    lbs = _lower_bounds(hgrn_lb_param)
    fg = final_g.reshape(1, D_MODEL).astype(F32)
    xp, xs = x_prompt, x_sample
    bp, sp, _ = xp.shape
    bs, ls, _ = xs.shape
    zero_hist = jnp.zeros((bp, CONV_W - 1, CA), F32)
    zero_s = jnp.zeros((bp, HC, DK, DV), F32)
    past = cache_fox_k.shape[2]
    k_cache_t = cache_fox_k.transpose(0, 1, 3, 4, 2).reshape(depth, bs, DB, past)
    v_cache_t = cache_fox_v.transpose(0, 1, 3, 4, 2).reshape(depth, bs, DB, past)
    w = _stacked_weights(p, lbs)
    stacks_p = stacks_s = None
    conv_p, conv_s, hgrn_p, hgrn_s = [], [], [], []
    for l in range(depth):
        final = l == depth - 1
        xp, stacks_p, cp, hp = _trunk_layer(
            xp, w, l, depth, stacks_p, fg, final, zero_hist, zero_s, None,
            tm_a=min(512, bp * sp), tm_d=min(256, bp * sp), tl=min(512, sp),
            tq=min(512, sp), hg_tile=min(512, sp))
        xs, stacks_s, cs, hs = _trunk_layer(
            xs, w, l, depth, stacks_s, fg, final, state_conv[l], state_hgrn[l],
            (k_cache_t, v_cache_t, cache_fox_logf[l]),
            tm_a=min(512, bs * ls), tm_d=min(256, bs * ls), tl=ls, tq=None, hg_tile=ls)
        conv_p.append(cp); conv_s.append(cs); hgrn_p.append(hp); hgrn_s.append(hs)
    kp, vp, lfp = stacks_p
    ks, vs, lfs = stacks_s
    heads_last = lambda a: a.reshape(depth, bp, HB, DHB, sp).transpose(0, 1, 4, 2, 3)
    return (xp, xs,
            heads_last(kp), heads_last(vp),
            lfp.reshape(depth, bp, sp, HB), jnp.stack(conv_p), jnp.stack(hgrn_p),
            ks.reshape(depth, bs, ls, HB, DHB), vs.reshape(depth, bs, ls, HB, DHB),
            lfs.reshape(depth, bs, ls, HB), jnp.stack(conv_s), jnp.stack(hgrn_s))
```

```python
import functools
import math

import numpy as np
import jax
import jax.numpy as jnp
from jax import lax
from jax.experimental import pallas as pl
from jax.experimental.pallas import tpu as pltpu

D_MODEL = 1024
CONV_W = 31
CA = 512
HB = 8
DHB = 64
DB = HB * DHB
HC = 4
DK = 128
DV = 128
DC = HC * DV
D_FF = 4 * D_MODEL
EPS = 1e-6
HGRN_CHUNK = 128

LANES = 128
SUBLANES = 8
VMEM_LIMIT = 56 * 1024 * 1024
NEG_BIG = -1e30
LOG2E = 1.4426950408889634

F32 = jnp.float32
BF16 = jnp.bfloat16

_OFF = np.cumsum([0, CA, CA, DB, DB, DB, HB, HC * DK, HC * DK, DC, DC, D_MODEL, D_MODEL, D_MODEL])
(O_AV, O_AG, O_QB, O_KB, O_VB, O_FB, O_QC, O_FC, O_IC, O_OC, O_GA, O_GB, O_GC, O_END) = [int(v) for v in _OFF]


def _dot(a, b):
    return jnp.dot(a, b, preferred_element_type=F32)


def _dot_nt(a, b):
    return lax.dot_general(a, b, (((1,), (1,)), ((), ())), preferred_element_type=F32)


def _dot_tn(a, b):
    return lax.dot_general(a, b, (((0,), (0,)), ((), ())), preferred_element_type=F32)


def _sigmoid(x):
    return 1.0 / (1.0 + jnp.exp(-x))


def _silu(x):
    return x * _sigmoid(x)


def _rms(x, g):
    return x * lax.rsqrt(jnp.mean(x * x, axis=-1, keepdims=True) + EPS) * g


def _split3(x):
    hi = x.astype(BF16)
    r = x - hi.astype(F32)
    mid = r.astype(BF16)
    lo = (r - mid.astype(F32)).astype(BF16)
    return hi, mid, lo


def _const_spec(shape):
    nd = len(shape)
    return pl.BlockSpec(shape, lambda *_: (0,) * nd, pipeline_mode=pl.Buffered(1))


def _layer_spec(shape, layer):
    nd = len(shape)
    return pl.BlockSpec((None,) + tuple(shape), lambda *_: (layer,) + (0,) * nd, pipeline_mode=pl.Buffered(1))


A_WIDTH = 9 * 512 + LANES
PAIR_LANES = 2 * LANES
KAUG_WIDTH = (HB // 2) * PAIR_LANES


def _bias_placement_matrix():
    m = np.zeros((3 * LANES, (HB // 2) * LANES), np.float32)
    for head in range(HB):
        pair, e = divmod(head, 2)
        for part in range(3):
            m[part * LANES + head, pair * LANES + 3 * e + part] = 1.0
    return m


def _in_proj_kernel(x_ref, g1_ref, w_ref, fbias_ref, lb_ref, tri_ref, place_ref, *rest,
                    tiles_per_seq, n_alias, feature_major):
    (u_ref, qs_ref, k_ref, kaug_ref, v_ref, vt_ref, lf_ref,
     qh_ref, lfc_ref, kk_ref, vv_ref, og_ref, carry_ref) = rest[n_alias:]
    i = pl.program_id(0)

    @pl.when(i % tiles_per_seq == 0)
    def _():
        carry_ref[...] = jnp.zeros_like(carry_ref)

    x = x_ref[...]
    h = _rms(x, g1_ref[...]).astype(BF16)

    def seg(i, width=512):
        return _dot(h, w_ref[:, i * 512:i * 512 + width])

    f = seg(9, LANES) + fbias_ref[...]
    lf = jnp.minimum(f, 0.0) - jnp.log(1.0 + jnp.exp(-jnp.abs(f)))
    lf_ref[...] = lf[:, :HB]

    hi, mid, lo = _split3(lf)
    c3 = _dot(tri_ref[...], jnp.concatenate([hi, mid, lo], axis=1))
    c = (c3[:, :LANES] + c3[:, LANES:2 * LANES]) + c3[:, 2 * LANES:] + carry_ref[0:1, :]
    carry_ref[0:1, :] = c[c.shape[0] - 1:, :]
    bh, bm, bl = _split3(-LOG2E * c)
    placed = _dot(jnp.concatenate([bh, bm, bl], axis=1), place_ref[...]).astype(BF16)

    z = seg(6)
    log_lb = lb_ref[0:1, :]
    log1m_lb = lb_ref[1:2, :]
    one_m_lb = lb_ref[2:3, :]
    e = jnp.exp(-jnp.abs(z))
    ls = jnp.minimum(z, 0.0) - jnp.log(1.0 + e)
    b = log1m_lb + ls
    mx = jnp.maximum(log_lb, b)
    lfc_ref[...] = mx + jnp.log(1.0 + jnp.exp(-jnp.abs(log_lb - b)))
    r = 1.0 / (1.0 + e)
    kk_ref[...] = (one_m_lb * jnp.where(z >= 0.0, e * r, r)).astype(BF16)

    u_ref[...] = seg(0) * _sigmoid(seg(1))
    qs_ref[...] = (seg(2) * (LOG2E * DHB ** -0.5)).astype(BF16)
    k = seg(3)
    for p in range(HB // 2):
        kaug_ref[:, 2 * p * LANES:(2 * p + 1) * LANES] = k[:, p * LANES:(p + 1) * LANES].astype(BF16)
        kaug_ref[:, (2 * p + 1) * LANES:(2 * p + 2) * LANES] = placed[:, p * LANES:(p + 1) * LANES]
    v = seg(4)
    v_t = v.T
    vt_ref[...] = v_t.astype(BF16)
    k_ref[...] = k.T if feature_major else k
    v_ref[...] = v_t if feature_major else v
    qh_ref[...] = _silu(seg(5)).astype(BF16)
    vv_ref[...] = seg(7).astype(BF16)
    og_ref[...] = _silu(seg(8)).astype(BF16)


def _in_proj(x, w, layer, depth, stacks, tm, tiles_per_seq, feature_major):
    n = x.shape[0]
    tok = lambda w_, dt: jax.ShapeDtypeStruct((n, w_), dt)
    row = lambda w_: pl.BlockSpec((tm, w_), lambda i: (i, 0))
    slab = lambda w_: pl.BlockSpec((None, tm, w_), lambda i: (layer, i, 0))
    stk = lambda w_: jax.ShapeDtypeStruct((depth, n, w_), F32)
    if feature_major:
        seq_len = tiles_per_seq * tm
        kv_stk = jax.ShapeDtypeStruct((depth, n // seq_len, DB, seq_len), F32)
        kv_slab = pl.BlockSpec((None, None, DB, tm),
                               lambda i: (layer, i // tiles_per_seq, 0, i % tiles_per_seq))
    else:
        kv_stk, kv_slab = stk(512), slab(512)
    tri = jnp.asarray(np.tril(np.ones((tm, tm), np.float32)), BF16)
    place = jnp.asarray(_bias_placement_matrix(), BF16)
    out_shape = (tok(512, F32), tok(512, BF16), kv_stk, tok(KAUG_WIDTH, BF16), kv_stk,
                 jax.ShapeDtypeStruct((n // tm, DB, tm), BF16), stk(HB),
                 tok(512, BF16), tok(512, F32), tok(512, BF16), tok(512, BF16), tok(512, BF16))
    out_specs = (row(512), row(512), kv_slab, row(KAUG_WIDTH), kv_slab,
                 pl.BlockSpec((None, DB, tm), lambda i: (i, 0, 0)), slab(HB),
                 row(512), row(512), row(512), row(512), row(512))
    in_specs = [row(D_MODEL), _layer_spec((1, D_MODEL), layer), _layer_spec((D_MODEL, A_WIDTH), layer),
                _layer_spec((1, LANES), layer), _layer_spec((8, 512), layer), _const_spec((tm, tm)),
                _const_spec(place.shape)]
    args = [x, w["g1"], w["w_a"], w["fbias"], w["lbrows"], tri, place]
    aliases = {}
    if stacks is not None:
        aliases = {len(args) + a: out_idx for a, out_idx in enumerate((2, 4, 6))}
        in_specs += [pl.BlockSpec(memory_space=pl.ANY)] * len(stacks)
        args += list(stacks)
    return pl.pallas_call(
        functools.partial(_in_proj_kernel, tiles_per_seq=tiles_per_seq, n_alias=len(aliases),
                          feature_major=feature_major),
        grid=(n // tm,),
        in_specs=in_specs,
        out_specs=out_specs,
        out_shape=out_shape,
        input_output_aliases=aliases,
        scratch_shapes=[pltpu.VMEM((8, LANES), F32)],
        compiler_params=pltpu.CompilerParams(dimension_semantics=("arbitrary",),
                                             vmem_limit_bytes=VMEM_LIMIT),
        name="in_proj",
    )(*args)


def _cumsum_kernel(lf_ref, hi_ref, mid_ref, lo_ref, *, n, pivot):
    x = lf_ref[...]
    lane = lax.broadcasted_iota(jnp.int32, x.shape, 1)
    s = 1
    while s < n:
        x = x + jnp.where(lane >= s, pltpu.roll(x, s, 1), 0.0)
        s *= 2
    piv = jnp.sum(jnp.where(lane == pivot, x, 0.0), axis=1, keepdims=True)
    hi, mid, lo = _split3(LOG2E * (piv - x))
    hi_ref[...] = hi
    mid_ref[...] = mid
    lo_ref[...] = lo


def _cumsum_bias(lf_t, pivot):
    bsz, hh, n = lf_t.shape
    spec = pl.BlockSpec((bsz * hh, n), lambda i: (0, 0))
    sds = jax.ShapeDtypeStruct((bsz * hh, n), BF16)
    outs = pl.pallas_call(
        functools.partial(_cumsum_kernel, n=n, pivot=pivot),
        grid=(1,),
        in_specs=[spec],
        out_specs=(spec, spec, spec),
        out_shape=(sds, sds, sds),
        name="cumsum_bias",
    )(lf_t.reshape(bsz * hh, n))
    return tuple(o.reshape(bsz, hh, n) for o in outs)


HIST_ROWS = 32
HIST_PAD = HIST_ROWS - (CONV_W - 1)


def _conv_kernel(u_ref, hist_ref, w_ref, cb_ref, g_ref, b_ref, feat_ref, new_ref, buf_ref, *, tl, rc):
    t = pl.program_id(1)

    @pl.when(t == 0)
    def _():
        buf_ref[0:HIST_ROWS, :] = hist_ref[0]

    @pl.when(t > 0)
    def _():
        buf_ref[0:HIST_ROWS, :] = buf_ref[tl:tl + HIST_ROWS, :]

    buf_ref[HIST_ROWS:HIST_ROWS + tl, :] = u_ref[0]
    new_ref[0] = buf_ref[tl:tl + HIST_ROWS, :]

    cb = cb_ref[...]
    g = g_ref[...]
    b = b_ref[...]
    for r0 in range(0, tl, rc):
        y = cb
        for r in range(SUBLANES):
            z = None
            for a in range((HIST_ROWS + SUBLANES) // SUBLANES):
                j = SUBLANES * a + r - HIST_PAD
                if 0 <= j < CONV_W:
                    nrows = rc + (SUBLANES if r else 0)
                    term = w_ref[j:j + 1, :] * buf_ref[r0 + SUBLANES * a:r0 + SUBLANES * a + nrows, :]
                    z = term if z is None else z + term
            y = y + z[r:r + rc]
        mu = jnp.mean(y, axis=-1, keepdims=True)
        yc = y - mu
        yn = yc * lax.rsqrt(jnp.mean(yc * yc, axis=-1, keepdims=True) + EPS) * g + b
        feat_ref[0, r0:r0 + rc, :] = _silu(yn).astype(BF16)


def _conv(u, hist, w, layer, tl):
    bsz, L, _ = u.shape
    rc = min(tl, 64)
    vec = _layer_spec((1, CA), layer)
    return pl.pallas_call(
        functools.partial(_conv_kernel, tl=tl, rc=rc),
        grid=(bsz, L // tl),
        in_specs=[pl.BlockSpec((1, tl, CA), lambda i, t: (i, t, 0)),
                  pl.BlockSpec((1, HIST_ROWS, CA), lambda i, t: (i, 0, 0)),
                  _layer_spec((HIST_ROWS, CA), layer), vec, vec, vec],
        out_specs=(pl.BlockSpec((1, tl, CA), lambda i, t: (i, t, 0)),
                   pl.BlockSpec((1, HIST_ROWS, CA), lambda i, t: (i, 0, 0))),
        out_shape=(jax.ShapeDtypeStruct((bsz, L, CA), BF16),
                   jax.ShapeDtypeStruct((bsz, HIST_ROWS, CA), F32)),
        scratch_shapes=[pltpu.VMEM((tl + HIST_ROWS, CA), F32)],
        compiler_params=pltpu.CompilerParams(dimension_semantics=("parallel", "arbitrary")),
        name="conv",
    )(u, hist, w["conv_w"], w["conv_b"], w["ln_g"], w["ln_b"])


ONES_ROWS = 16


def _attn_kernel(q_ref, k_ref, vt_ref, o_ref, *, tq):
    i = pl.program_id(1)
    lane = lax.broadcasted_iota(jnp.int32, (tq, LANES), 1)
    qm = []
    for p in range(HB // 2):
        q = q_ref[:, p * LANES:(p + 1) * LANES]
        for e in range(2):
            qa = jnp.where(lane // DHB == e, q, jnp.zeros_like(q))
            pick = jnp.where((lane >= 3 * e) & (lane < 3 * e + 3), 1.0, 0.0).astype(BF16)
            qm.append(jnp.concatenate([qa, pick], axis=1))
    ones = jnp.ones((ONES_ROWS, tq), BF16)

    def step(j, carry, diag):
        rows = pl.ds(pl.multiple_of(j * tq, tq), tq)
        if diag:
            kk = lax.broadcasted_iota(jnp.int32, (tq, tq), 0)
            qq = lax.broadcasted_iota(jnp.int32, (tq, tq), 1)
            keep = kk <= qq
        ss = []
        for p in range(HB // 2):
            kblk = k_ref[rows, p * PAIR_LANES:(p + 1) * PAIR_LANES]
            for e in range(2):
                ss.append(_dot_nt(kblk, qm[2 * p + e]))
        if diag:
            ss = [jnp.where(keep, s, NEG_BIG) for s in ss]
        ms = [jnp.maximum(carry[h][0], jnp.max(ss[h], axis=0, keepdims=True)) for h in range(HB)]
        out = []
        for h in range(HB):
            m, acc = carry[h]
            alpha = jnp.exp2(m - ms[h])
            p = jnp.exp2(ss[h] - ms[h]).astype(BF16)
            vt = jnp.concatenate([vt_ref[j, h * DHB:(h + 1) * DHB, :], ones], axis=0)
            out.append((ms[h], acc * alpha + _dot(vt, p)))
        return tuple(out)

    init = tuple((jnp.full((1, tq), NEG_BIG, F32), jnp.zeros((DHB + ONES_ROWS, tq), F32))
                 for _ in range(HB))
    carry = lax.fori_loop(0, i, lambda j, c: step(j, c, False), init)
    carry = step(i, carry, True)
    o_t = jnp.concatenate([acc[:DHB] / acc[DHB:DHB + 1] for _, acc in carry], axis=0)
    o_ref[...] = o_t.T.astype(BF16)


def _attn_prompt(qs, kaug, vt, bsz, S, tq):
    nq = S // tq
    once = pl.Buffered(1)
    return pl.pallas_call(
        functools.partial(_attn_kernel, tq=tq),
        grid=(bsz, nq),
        in_specs=[pl.BlockSpec((tq, DB), lambda b, i: (b * nq + i, 0)),
                  pl.BlockSpec((S, KAUG_WIDTH), lambda b, i: (b, 0), pipeline_mode=once),
                  pl.BlockSpec((nq, DB, tq), lambda b, i: (b, 0, 0), pipeline_mode=once)],
        out_specs=pl.BlockSpec((tq, DB), lambda b, i: (b * nq + i, 0)),
        out_shape=jax.ShapeDtypeStruct((bsz * S, DB), BF16),
        compiler_params=pltpu.CompilerParams(dimension_semantics=("parallel", "arbitrary"),
                                             vmem_limit_bytes=VMEM_LIMIT),
        name="attn_prompt",
    )(qs, kaug, vt)


def _attn_cached_kernel(q_ref, kc_ref, vc_ref, kn_ref, vn_ref, bhi_ref, bmid_ref, blo_ref, o_ref, *, L, P):
    q = q_ref[0]
    lane_head = lax.broadcasted_iota(jnp.int32, (L, DB), 1) // DHB
    q_bd = jnp.concatenate([jnp.where(lane_head == h, q, jnp.zeros_like(q)) for h in range(HB)], axis=0)
    kc = kc_ref[...].astype(BF16)
    vc = vc_ref[...].astype(BF16)
    zpad = jnp.zeros((LANES - L, DB), BF16)
    kn = jnp.concatenate([kn_ref[0].astype(BF16), zpad], axis=0)
    vn = jnp.concatenate([vn_ref[0].astype(BF16), zpad], axis=0)
    bias = (bhi_ref[0].astype(F32) + bmid_ref[0].astype(F32)) + blo_ref[0].astype(F32)

    s_c = _dot(q_bd, kc)
    s_n = _dot_nt(q_bd, kn)
    tt = lax.broadcasted_iota(jnp.int32, (L, LANES), 0)
    uu = lax.broadcasted_iota(jnp.int32, (L, LANES), 1)
    causal = uu <= tt
    pcs, pns, ls = [], [], []
    for h in range(HB):
        sc = s_c[h * L:(h + 1) * L, :] + bias[h:h + 1, :P]
        sn = jnp.where(causal, s_n[h * L:(h + 1) * L, :] + bias[h:h + 1, P:], NEG_BIG)
        m = jnp.maximum(jnp.max(sc, axis=1, keepdims=True), jnp.max(sn, axis=1, keepdims=True))
        pc = jnp.exp2(sc - m)
        pn = jnp.exp2(sn - m)
        ls.append(jnp.sum(pc, axis=1, keepdims=True) + jnp.sum(pn, axis=1, keepdims=True))
        pcs.append(pc.astype(BF16))
        pns.append(pn.astype(BF16))
    o_all = _dot_nt(jnp.concatenate(pcs, axis=0), vc) + _dot(jnp.concatenate(pns, axis=0), vn)
    out = jnp.zeros((L, DB), F32)
    for h in range(HB):
        out = out + jnp.where(lane_head == h, o_all[h * L:(h + 1) * L, :] / ls[h], 0.0)
    o_ref[0] = out.astype(BF16)


def _attn_cached(qs, k_cache_t, v_cache_t, layer, k_new, v_new, bhi, bmid, blo):
    bsz, L, _ = qs.shape
    P = k_cache_t.shape[-1]
    seq = lambda r, w: pl.BlockSpec((1, r, w), lambda b: (b, 0, 0))
    cache = pl.BlockSpec((None, None, DB, P), lambda b: (layer, b, 0, 0))
    return pl.pallas_call(
        functools.partial(_attn_cached_kernel, L=L, P=P),
        grid=(bsz,),
        in_specs=[seq(L, DB), cache, cache, seq(L, DB), seq(L, DB),
                  seq(HB, P + LANES), seq(HB, P + LANES), seq(HB, P + LANES)],
        out_specs=seq(L, DB),
        out_shape=jax.ShapeDtypeStruct((bsz, L, DB), BF16),
        compiler_params=pltpu.CompilerParams(dimension_semantics=("parallel",),
                                             vmem_limit_bytes=VMEM_LIMIT),
        name="attn_cached",
    )(qs, k_cache_t, v_cache_t, k_new, v_new, bhi, bmid, blo)


def _hgrn_exponent_matrix(C):
    nl = int(math.log2(C))
    rows = []
    idx = np.arange(C)
    for lv in range(nl):
        sz = 1 << lv
        m = np.zeros((C, C), np.float32)
        for t in range(C):
            bnd = ((t >> (lv + 1)) << (lv + 1)) + sz - 1
            if (t >> lv) & 1:
                m[t, (idx > bnd) & (idx <= t)] = 1.0
            else:
                m[t, (idx > t) & (idx <= bnd)] = 1.0
        rows.append(m)
    rows.append((idx[None, :] <= idx[:, None]).astype(np.float32))
    rows.append((idx[None, :] > idx[:, None]).astype(np.float32))
    p = np.concatenate(rows, axis=0)
    return np.concatenate([p, p], axis=1)


def _hgrn_kernel(p_ref, qh_ref, lf_ref, kk_ref, vv_ref, og_ref, s0_ref, g_ref, hg_ref, sout_ref, st_ref,
                 *, C, T):
    t = pl.program_id(1)
    nl = int(math.log2(C))

    @pl.when(t == 0)
    def _():
        for h in range(HC):
            st_ref[h] = s0_ref[0, h].T

    row = lax.broadcasted_iota(jnp.int32, (C, C), 0)
    col = lax.broadcasted_iota(jnp.int32, (C, C), 1)
    diff = row ^ col
    masks = [(jnp.right_shift(diff, lv) == 1) & ((jnp.right_shift(row, lv) & 1) == 1) for lv in range(nl)]
    diag = row == col
    g = g_ref[...]
    pmat = p_ref[...]

    pre = {}
    for c0 in range(0, T, C):
        lf = LOG2E * lf_ref[0, c0:c0 + C, :]
        hi = lf.astype(BF16)
        lo = (lf - hi.astype(F32)).astype(BF16)
        e_all = jnp.exp2(_dot(pmat, jnp.concatenate([hi, lo], axis=0)))
        for h in range(HC):
            hs = slice(h * DK, (h + 1) * DK)
            qb = qh_ref[0, c0:c0 + C, hs]
            kb = kk_ref[0, c0:c0 + C, hs]
            q = qb.astype(F32)
            k = kb.astype(F32)
            k_t = k.T
            a = jnp.where(diag, _dot(qb, k_t.astype(BF16)), 0.0)
            for lv in range(nl):
                e = e_all[lv * C:(lv + 1) * C, hs]
                a = jnp.where(masks[lv], _dot((q * e).astype(BF16), (k_t * e.T).astype(BF16)), a)
            e_q = e_all[nl * C:(nl + 1) * C, hs]
            e_k = e_all[(nl + 1) * C:(nl + 2) * C, hs]
            pre[c0, h] = (a.astype(BF16), (q * e_q).astype(BF16), (k * e_k).astype(BF16), e_q[C - 1:C, :])

    for h in range(HC):
        hs = slice(h * DK, (h + 1) * DK)
        st = st_ref[h]
        for c0 in range(0, T, C):
            a, q_dec, k_dec, chunk_dec = pre[c0, h]
            v = vv_ref[0, c0:c0 + C, hs]
            o = _dot(a, v) + _dot_nt(q_dec, st.astype(BF16))
            st = st * chunk_dec + _dot_tn(v, k_dec)
            o = o * lax.rsqrt(jnp.mean(o * o, axis=-1, keepdims=True) + EPS) * g
            hg_ref[0, c0:c0 + C, hs] = (o * og_ref[0, c0:c0 + C, hs].astype(F32)).astype(BF16)
        st_ref[h] = st

    @pl.when(t == pl.num_programs(1) - 1)
    def _():
        for h in range(HC):
            sout_ref[0, h] = st_ref[h].T


def _hgrn(qh, lfc, kk, vv, og, s0, g, layer, C, T):
    bsz, L, _ = qh.shape
    pmat = jnp.asarray(_hgrn_exponent_matrix(C), BF16)
    tile = pl.BlockSpec((1, T, DC), lambda b, t: (b, t, 0))
    state = pl.BlockSpec((1, HC, DK, DV), lambda b, t: (b, 0, 0, 0))
    return pl.pallas_call(
        functools.partial(_hgrn_kernel, C=C, T=T),
        grid=(bsz, L // T),
        in_specs=[_const_spec(pmat.shape), tile, tile, tile, tile, tile, state, _layer_spec((1, DV), layer)],
        out_specs=(tile, state),
        out_shape=(jax.ShapeDtypeStruct((bsz, L, DC), BF16),
                   jax.ShapeDtypeStruct((bsz, HC, DK, DV), F32)),
        scratch_shapes=[pltpu.VMEM((HC, DV, DK), F32)],
        compiler_params=pltpu.CompilerParams(dimension_semantics=("parallel", "arbitrary")),
        name="hgrn",
    )(pmat, qh, lfc, kk, vv, og, s0, g)


FF_CHUNK = 1024


def _merge_ffn_kernel(x_ref, fa_ref, fb_ref, fc_ref, g1_ref, wg_ref, wa_ref, wb_ref, wc_ref, wo_ref,
                      g2_ref, wup_ref, wdn_ref, gf_ref, o_ref, *, final):
    x = x_ref[...]
    h1 = _rms(x, g1_ref[...]).astype(BF16)
    m = None
    for i, (f_ref, w_ref) in enumerate(((fa_ref, wa_ref), (fb_ref, wb_ref), (fc_ref, wc_ref))):
        gate = _sigmoid(_dot(h1, wg_ref[:, i * D_MODEL:(i + 1) * D_MODEL]))
        y = gate * _dot(f_ref[...], w_ref[...])
        m = y if m is None else m + y
    x = x + _dot(m.astype(BF16), wo_ref[...])
    h2 = _rms(x, g2_ref[...]).astype(BF16)
    acc = None
    for c in range(0, D_FF, FF_CHUNK):
        up = jnp.maximum(_dot(h2, wup_ref[:, c:c + FF_CHUNK]), 0.0)
        d = _dot((up * up).astype(BF16), wdn_ref[c:c + FF_CHUNK, :])
        acc = d if acc is None else acc + d
    x = x + acc
    if final:
        x = _rms(x, gf_ref[...])
    o_ref[...] = x


def _merge_ffn(x, fa, fb, fc, w, layer, final_g, tm, final):
    n = x.shape[0]
    row = lambda w_: pl.BlockSpec((tm, w_), lambda i: (i, 0))
    lw = lambda *shape: _layer_spec(shape, layer)
    return pl.pallas_call(
        functools.partial(_merge_ffn_kernel, final=final),
        grid=(n // tm,),
        in_specs=[row(D_MODEL), row(CA), row(DB), row(DC),
                  lw(1, D_MODEL), lw(D_MODEL, 3 * D_MODEL),
                  lw(CA, D_MODEL), lw(DB, D_MODEL), lw(DC, D_MODEL),
                  lw(D_MODEL, D_MODEL), lw(1, D_MODEL),
                  lw(D_MODEL, D_FF), lw(D_FF, D_MODEL), _const_spec((1, D_MODEL))],
        out_specs=row(D_MODEL),
        out_shape=jax.ShapeDtypeStruct((n, D_MODEL), F32),
        compiler_params=pltpu.CompilerParams(dimension_semantics=("parallel",),
                                             vmem_limit_bytes=VMEM_LIMIT),
        name="merge_ffn",
    )(x, fa, fb, fc, w["g1"], w["w_gate"], w["w_a_out"], w["w_b_out"], w["w_c_out"], w["w_o"],
      w["g2"], w["w_up"], w["w_down"], final_g)


def _stacked_weights(p, lbs):
    w_in = p["w_in"]
    cols = lambda a, b: w_in[:, :, a:b]
    w_a = jnp.concatenate([cols(O_AV, O_QB), cols(O_QB, O_FB), cols(O_QC, O_GA),
                           jnp.pad(cols(O_FB, O_QC), ((0, 0), (0, 0), (0, LANES - HB)))], axis=2).astype(BF16)
    row = lambda v: v[:, None, :].astype(F32)
    lbrows = jnp.concatenate([jnp.log(lbs)[:, None], jnp.log1p(-lbs)[:, None], (1.0 - lbs)[:, None],
                              jnp.zeros((lbs.shape[0], 5, HC * DK), F32)], axis=1)
    return dict(
        w_a=w_a, g1=row(p["norm1_g"]),
        fbias=row(jnp.pad(p["fox_bf"], ((0, 0), (0, LANES - HB)))),
        lbrows=lbrows,
        conv_w=jnp.pad(p["conv_w"], ((0, 0), (0, HIST_ROWS - CONV_W), (0, 0))),
        conv_b=row(p["conv_b"]), ln_g=row(p["conv_ln_g"]), ln_b=row(p["conv_ln_b"]),
        hn_g=row(p["hgrn_norm_g"]),
        w_gate=cols(O_GA, O_END).astype(BF16),
        w_a_out=p["w_a_out"].astype(BF16), w_b_out=p["w_b_out"].astype(BF16),
        w_c_out=p["w_c_out"].astype(BF16), w_o=p["w_o"].astype(BF16),
        g2=row(p["norm2_g"]), w_up=p["w_up"].astype(BF16), w_down=p["w_down"].astype(BF16),
    )


def _pad_hist(h):
    return jnp.pad(h, ((0, 0), (HIST_PAD, 0), (0, 0)))


def _trunk_layer(x, w, layer, depth, stacks, final_g, final, conv_hist, s0, cache,
                 *, tm_a, tm_d, tl, tq, hg_tile):
    bsz, L, _ = x.shape
    n = bsz * L
    xf = x.reshape(n, D_MODEL)
    u, qs, k_all, kaug, v_all, vt, lf_all, qh, lfc, kk, vv, og = _in_proj(
        xf, w, layer, depth, stacks, tm_a, max(L // tm_a, 1), feature_major=cache is None)
    seq = lambda a: a.reshape(bsz, L, a.shape[-1])

    feat, conv_new = _conv(seq(u), _pad_hist(conv_hist), w, layer, tl)

    if cache is None:
        o_b = _attn_prompt(qs, kaug, vt, bsz, L, tq)
    else:
        k_cache_t, v_cache_t, lf_cache = cache
        P = k_cache_t.shape[-1]
        lf_t = seq(lf_all[layer]).transpose(0, 2, 1)
        lf_cat = jnp.concatenate([lf_cache.transpose(0, 2, 1), lf_t,
                                  jnp.zeros((bsz, HB, LANES - L), F32)], axis=-1)
        hi, mid, lo = _cumsum_bias(lf_cat, pivot=P - 1)
        o_b = _attn_cached(seq(qs), k_cache_t, v_cache_t, layer,
                           seq(k_all[layer]), seq(v_all[layer]), hi, mid, lo).reshape(n, DB)

    C = min(HGRN_CHUNK, L)
    hg, s_new = _hgrn(seq(qh), seq(lfc), seq(kk), seq(vv), seq(og), s0, w["hn_g"], layer, C, hg_tile)

    x_new = _merge_ffn(xf, feat.reshape(n, CA), o_b, hg.reshape(n, DC), w, layer, final_g, tm_d, final)
    return x_new.reshape(bsz, L, D_MODEL), (k_all, v_all, lf_all), conv_new[:, HIST_PAD:], s_new


def _lower_bounds(p):
    s = jax.nn.softmax(p.astype(F32), axis=0)
    return jnp.maximum(jnp.cumsum(s, axis=0) - s[0], 0.0)


def kernel(x_prompt, x_sample, cache_fox_k, cache_fox_v, cache_fox_logf, state_conv, state_hgrn,
           norm1_g, w_in, conv_w, conv_b, conv_ln_g, conv_ln_b, w_a_out, fox_bf, w_b_out,
           hgrn_lb_param, hgrn_norm_g, w_c_out, w_o, norm2_g, w_up, w_down, final_g):
    p = dict(norm1_g=norm1_g, w_in=w_in, conv_w=conv_w, conv_b=conv_b, conv_ln_g=conv_ln_g,
             conv_ln_b=conv_ln_b, w_a_out=w_a_out, fox_bf=fox_bf, w_b_out=w_b_out,
             hgrn_norm_g=hgrn_norm_g, w_c_out=w_c_out, w_o=w_o, norm2_g=norm2_g, w_up=w_up, w_down=w_down)
    depth = w_in.shape[0]
    lbs = _lower_bounds(hgrn_lb_param)
    fg = final_g.reshape(1, D_MODEL).astype(F32)
    xp, xs = x_prompt, x_sample
    bp, sp, _ = xp.shape
    bs, ls, _ = xs.shape
    zero_hist = jnp.zeros((bp, CONV_W - 1, CA), F32)
    zero_s = jnp.zeros((bp, HC, DK, DV), F32)
    past = cache_fox_k.shape[2]
    k_cache_t = cache_fox_k.transpose(0, 1, 3, 4, 2).reshape(depth, bs, DB, past)
    v_cache_t = cache_fox_v.transpose(0, 1, 3, 4, 2).reshape(depth, bs, DB, past)
    w = _stacked_weights(p, lbs)
    stacks_p = stacks_s = None
    conv_p, conv_s, hgrn_p, hgrn_s = [], [], [], []
    for l in range(depth):
        final = l == depth - 1
        xp, stacks_p, cp, hp = _trunk_layer(
            xp, w, l, depth, stacks_p, fg, final, zero_hist, zero_s, None,
            tm_a=min(512, bp * sp), tm_d=min(512, bp * sp), tl=min(512, sp),
            tq=min(512, sp), hg_tile=min(512, sp))
        xs, stacks_s, cs, hs = _trunk_layer(
            xs, w, l, depth, stacks_s, fg, final, state_conv[l], state_hgrn[l],
            (k_cache_t, v_cache_t, cache_fox_logf[l]),
            tm_a=min(512, bs * ls), tm_d=min(256, bs * ls), tl=ls, tq=None, hg_tile=ls)
        conv_p.append(cp); conv_s.append(cs); hgrn_p.append(hp); hgrn_s.append(hs)
    kp, vp, lfp = stacks_p
    ks, vs, lfs = stacks_s
    heads_last = lambda a: a.reshape(depth, bp, HB, DHB, sp).transpose(0, 1, 4, 2, 3)
    return (xp, xs,
            heads_last(kp), heads_last(vp),
            lfp.reshape(depth, bp, sp, HB), jnp.stack(conv_p), jnp.stack(hgrn_p),
            ks.reshape(depth, bs, ls, HB, DHB), vs.reshape(depth, bs, ls, HB, DHB),
            lfs.reshape(depth, bs, ls, HB), jnp.stack(conv_s), jnp.stack(hgrn_s))
```

```python
import functools
import math

import numpy as np
import jax
import jax.numpy as jnp
from jax import lax
from jax.experimental import pallas as pl
from jax.experimental.pallas import tpu as pltpu

D_MODEL = 1024
CONV_W = 31
CA = 512
HB = 8
DHB = 64
DB = HB * DHB
HC = 4
DK = 128
DV = 128
DC = HC * DV
D_FF = 4 * D_MODEL
EPS = 1e-6
HGRN_CHUNK = 128

LANES = 128
SUBLANES = 8
VMEM_LIMIT = 56 * 1024 * 1024
NEG_BIG = -1e30
LOG2E = 1.4426950408889634

F32 = jnp.float32
BF16 = jnp.bfloat16

_OFF = np.cumsum([0, CA, CA, DB, DB, DB, HB, HC * DK, HC * DK, DC, DC, D_MODEL, D_MODEL, D_MODEL])
(O_AV, O_AG, O_QB, O_KB, O_VB, O_FB, O_QC, O_FC, O_IC, O_OC, O_GA, O_GB, O_GC, O_END) = [int(v) for v in _OFF]


def _dot(a, b):
    return jnp.dot(a, b, preferred_element_type=F32)


def _dot_nt(a, b):
    return lax.dot_general(a, b, (((1,), (1,)), ((), ())), preferred_element_type=F32)


def _dot_tn(a, b):
    return lax.dot_general(a, b, (((0,), (0,)), ((), ())), preferred_element_type=F32)


def _sigmoid(x):
    return 1.0 / (1.0 + jnp.exp(-x))


def _silu(x):
    return x * _sigmoid(x)


def _rms(x, g):
    return x * lax.rsqrt(jnp.mean(x * x, axis=-1, keepdims=True) + EPS) * g


def _split3(x):
    hi = x.astype(BF16)
    r = x - hi.astype(F32)
    mid = r.astype(BF16)
    lo = (r - mid.astype(F32)).astype(BF16)
    return hi, mid, lo


def _const_spec(shape):
    nd = len(shape)
    return pl.BlockSpec(shape, lambda *_: (0,) * nd, pipeline_mode=pl.Buffered(1))


def _layer_spec(shape, layer):
    nd = len(shape)
    return pl.BlockSpec((None,) + tuple(shape), lambda *_: (layer,) + (0,) * nd, pipeline_mode=pl.Buffered(1))


A_WIDTH = 9 * 512 + LANES
PAIR_LANES = 2 * LANES
KAUG_WIDTH = (HB // 2) * PAIR_LANES


def _bias_placement_matrix():
    m = np.zeros((3 * LANES, (HB // 2) * LANES), np.float32)
    for head in range(HB):
        pair, e = divmod(head, 2)
        for part in range(3):
            m[part * LANES + head, pair * LANES + 3 * e + part] = 1.0
    return m


def _in_proj_kernel(x_ref, g1_ref, w_ref, fbias_ref, lb_ref, tri_ref, place_ref, *rest,
                    tiles_per_seq, n_alias, feature_major):
    (u_ref, qs_ref, k_ref, kaug_ref, v_ref, vt_ref, lf_ref,
     qh_ref, lfc_ref, kk_ref, vv_ref, og_ref, carry_ref) = rest[n_alias:]
    i = pl.program_id(0)

    @pl.when(i % tiles_per_seq == 0)
    def _():
        carry_ref[...] = jnp.zeros_like(carry_ref)

    x = x_ref[...]
    h = _rms(x, g1_ref[...]).astype(BF16)

    def seg(i, width=512):
        return _dot(h, w_ref[:, i * 512:i * 512 + width])

    f = seg(9, LANES) + fbias_ref[...]
    lf = jnp.minimum(f, 0.0) - jnp.log(1.0 + jnp.exp(-jnp.abs(f)))
    lf_ref[...] = lf[:, :HB]

    hi, mid, lo = _split3(lf)
    c3 = _dot(tri_ref[...], jnp.concatenate([hi, mid, lo], axis=1))
    c = (c3[:, :LANES] + c3[:, LANES:2 * LANES]) + c3[:, 2 * LANES:] + carry_ref[0:1, :]
    carry_ref[0:1, :] = c[c.shape[0] - 1:, :]
    bh, bm, bl = _split3(-LOG2E * c)
    placed = _dot(jnp.concatenate([bh, bm, bl], axis=1), place_ref[...]).astype(BF16)

    z = seg(6)
    log_lb = lb_ref[0:1, :]
    log1m_lb = lb_ref[1:2, :]
    one_m_lb = lb_ref[2:3, :]
    e = jnp.exp(-jnp.abs(z))
    ls = jnp.minimum(z, 0.0) - jnp.log(1.0 + e)
    b = log1m_lb + ls
    mx = jnp.maximum(log_lb, b)
    lfc_ref[...] = mx + jnp.log(1.0 + jnp.exp(-jnp.abs(log_lb - b)))
    r = 1.0 / (1.0 + e)
    kk_ref[...] = (one_m_lb * jnp.where(z >= 0.0, e * r, r)).astype(BF16)

    u_ref[...] = seg(0) * _sigmoid(seg(1))
    qs_ref[...] = (seg(2) * (LOG2E * DHB ** -0.5)).astype(BF16)
    k = seg(3)
    for p in range(HB // 2):
        kaug_ref[:, 2 * p * LANES:(2 * p + 1) * LANES] = k[:, p * LANES:(p + 1) * LANES].astype(BF16)
        kaug_ref[:, (2 * p + 1) * LANES:(2 * p + 2) * LANES] = placed[:, p * LANES:(p + 1) * LANES]
    v = seg(4)
    v_t = v.T
    vt_ref[...] = v_t.astype(BF16)
    k_ref[...] = k.T if feature_major else k
    v_ref[...] = v_t if feature_major else v
    qh_ref[...] = _silu(seg(5)).astype(BF16)
    vv_ref[...] = seg(7).astype(BF16)
    og_ref[...] = _silu(seg(8)).astype(BF16)


def _in_proj(x, w, layer, depth, stacks, tm, tiles_per_seq, feature_major):
    n = x.shape[0]
    tok = lambda w_, dt: jax.ShapeDtypeStruct((n, w_), dt)
    row = lambda w_: pl.BlockSpec((tm, w_), lambda i: (i, 0))
    slab = lambda w_: pl.BlockSpec((None, tm, w_), lambda i: (layer, i, 0))
    stk = lambda w_: jax.ShapeDtypeStruct((depth, n, w_), F32)
    if feature_major:
        seq_len = tiles_per_seq * tm
        kv_stk = jax.ShapeDtypeStruct((depth, n // seq_len, DB, seq_len), F32)
        kv_slab = pl.BlockSpec((None, None, DB, tm),
                               lambda i: (layer, i // tiles_per_seq, 0, i % tiles_per_seq))
    else:
        kv_stk, kv_slab = stk(512), slab(512)
    tri = jnp.asarray(np.tril(np.ones((tm, tm), np.float32)), BF16)
    place = jnp.asarray(_bias_placement_matrix(), BF16)
    out_shape = (tok(512, F32), tok(512, BF16), kv_stk, tok(KAUG_WIDTH, BF16), kv_stk,
                 jax.ShapeDtypeStruct((n // tm, DB, tm), BF16), stk(HB),
                 tok(512, BF16), tok(512, F32), tok(512, BF16), tok(512, BF16), tok(512, BF16))
    out_specs = (row(512), row(512), kv_slab, row(KAUG_WIDTH), kv_slab,
                 pl.BlockSpec((None, DB, tm), lambda i: (i, 0, 0)), slab(HB),
                 row(512), row(512), row(512), row(512), row(512))
    in_specs = [row(D_MODEL), _layer_spec((1, D_MODEL), layer), _layer_spec((D_MODEL, A_WIDTH), layer),
                _layer_spec((1, LANES), layer), _layer_spec((8, 512), layer), _const_spec((tm, tm)),
                _const_spec(place.shape)]
    args = [x, w["g1"], w["w_a"], w["fbias"], w["lbrows"], tri, place]
    aliases = {}
    if stacks is not None:
        aliases = {len(args) + a: out_idx for a, out_idx in enumerate((2, 4, 6))}
        in_specs += [pl.BlockSpec(memory_space=pl.ANY)] * len(stacks)
        args += list(stacks)
    return pl.pallas_call(
        functools.partial(_in_proj_kernel, tiles_per_seq=tiles_per_seq, n_alias=len(aliases),
                          feature_major=feature_major),
        grid=(n // tm,),
        in_specs=in_specs,
        out_specs=out_specs,
        out_shape=out_shape,
        input_output_aliases=aliases,
        scratch_shapes=[pltpu.VMEM((8, LANES), F32)],
        compiler_params=pltpu.CompilerParams(dimension_semantics=("arbitrary",),
                                             vmem_limit_bytes=VMEM_LIMIT),
        name="in_proj",
    )(*args)


def _cumsum_kernel(lf_ref, hi_ref, mid_ref, lo_ref, *, n, pivot):
    x = lf_ref[...]
    lane = lax.broadcasted_iota(jnp.int32, x.shape, 1)
    s = 1
    while s < n:
        x = x + jnp.where(lane >= s, pltpu.roll(x, s, 1), 0.0)
        s *= 2
    piv = jnp.sum(jnp.where(lane == pivot, x, 0.0), axis=1, keepdims=True)
    hi, mid, lo = _split3(LOG2E * (piv - x))
    hi_ref[...] = hi
    mid_ref[...] = mid
    lo_ref[...] = lo


def _cumsum_bias(lf_t, pivot):
    bsz, hh, n = lf_t.shape
    spec = pl.BlockSpec((bsz * hh, n), lambda i: (0, 0))
    sds = jax.ShapeDtypeStruct((bsz * hh, n), BF16)
    outs = pl.pallas_call(
        functools.partial(_cumsum_kernel, n=n, pivot=pivot),
        grid=(1,),
        in_specs=[spec],
        out_specs=(spec, spec, spec),
        out_shape=(sds, sds, sds),
        name="cumsum_bias",
    )(lf_t.reshape(bsz * hh, n))
    return tuple(o.reshape(bsz, hh, n) for o in outs)


HIST_ROWS = 32
HIST_PAD = HIST_ROWS - (CONV_W - 1)


def _conv_kernel(u_ref, hist_ref, w_ref, cb_ref, g_ref, b_ref, feat_ref, new_ref, buf_ref, *, tl, rc):
    t = pl.program_id(1)

    @pl.when(t == 0)
    def _():
        buf_ref[0:HIST_ROWS, :] = hist_ref[0]

    @pl.when(t > 0)
    def _():
        buf_ref[0:HIST_ROWS, :] = buf_ref[tl:tl + HIST_ROWS, :]

    buf_ref[HIST_ROWS:HIST_ROWS + tl, :] = u_ref[0]
    new_ref[0] = buf_ref[tl:tl + HIST_ROWS, :]

    cb = cb_ref[...]
    g = g_ref[...]
    b = b_ref[...]
    for r0 in range(0, tl, rc):
        y = cb
        for r in range(SUBLANES):
            z = None
            for a in range((HIST_ROWS + SUBLANES) // SUBLANES):
                j = SUBLANES * a + r - HIST_PAD
                if 0 <= j < CONV_W:
                    nrows = rc + (SUBLANES if r else 0)
                    term = w_ref[j:j + 1, :] * buf_ref[r0 + SUBLANES * a:r0 + SUBLANES * a + nrows, :]
                    z = term if z is None else z + term
            y = y + z[r:r + rc]
        mu = jnp.mean(y, axis=-1, keepdims=True)
        yc = y - mu
        yn = yc * lax.rsqrt(jnp.mean(yc * yc, axis=-1, keepdims=True) + EPS) * g + b
        feat_ref[0, r0:r0 + rc, :] = _silu(yn).astype(BF16)


def _conv(u, hist, w, layer, tl):
    bsz, L, _ = u.shape
    rc = min(tl, 64)
    vec = _layer_spec((1, CA), layer)
    return pl.pallas_call(
        functools.partial(_conv_kernel, tl=tl, rc=rc),
        grid=(bsz, L // tl),
        in_specs=[pl.BlockSpec((1, tl, CA), lambda i, t: (i, t, 0)),
                  pl.BlockSpec((1, HIST_ROWS, CA), lambda i, t: (i, 0, 0)),
                  _layer_spec((HIST_ROWS, CA), layer), vec, vec, vec],
        out_specs=(pl.BlockSpec((1, tl, CA), lambda i, t: (i, t, 0)),
                   pl.BlockSpec((1, HIST_ROWS, CA), lambda i, t: (i, 0, 0))),
        out_shape=(jax.ShapeDtypeStruct((bsz, L, CA), BF16),
                   jax.ShapeDtypeStruct((bsz, HIST_ROWS, CA), F32)),
        scratch_shapes=[pltpu.VMEM((tl + HIST_ROWS, CA), F32)],
        compiler_params=pltpu.CompilerParams(dimension_semantics=("parallel", "arbitrary")),
        name="conv",
    )(u, hist, w["conv_w"], w["conv_b"], w["ln_g"], w["ln_b"])


ONES_ROWS = 16


def _attn_kernel(q_ref, k_ref, vt_ref, o_ref, *, tq):
    i = pl.program_id(1)
    lane = lax.broadcasted_iota(jnp.int32, (tq, LANES), 1)
    qm = []
    for p in range(HB // 2):
        q = q_ref[:, p * LANES:(p + 1) * LANES]
        for e in range(2):
            qa = jnp.where(lane // DHB == e, q, jnp.zeros_like(q))
            pick = jnp.where((lane >= 3 * e) & (lane < 3 * e + 3), 1.0, 0.0).astype(BF16)
            qm.append(jnp.concatenate([qa, pick], axis=1))
    ones = jnp.ones((ONES_ROWS, tq), BF16)

    def step(j, carry, diag):
        rows = pl.ds(pl.multiple_of(j * tq, tq), tq)
        if diag:
            kk = lax.broadcasted_iota(jnp.int32, (tq, tq), 0)
            qq = lax.broadcasted_iota(jnp.int32, (tq, tq), 1)
            keep = kk <= qq
        ss = []
        for p in range(HB // 2):
            kblk = k_ref[rows, p * PAIR_LANES:(p + 1) * PAIR_LANES]
            for e in range(2):
                ss.append(_dot_nt(kblk, qm[2 * p + e]))
        if diag:
            ss = [jnp.where(keep, s, NEG_BIG) for s in ss]
        ms = [jnp.maximum(carry[h][0], jnp.max(ss[h], axis=0, keepdims=True)) for h in range(HB)]
        out = []
        for h in range(HB):
            m, acc = carry[h]
            alpha = jnp.exp2(m - ms[h])
            p = jnp.exp2(ss[h] - ms[h]).astype(BF16)
            vt = jnp.concatenate([vt_ref[j, h * DHB:(h + 1) * DHB, :], ones], axis=0)
            out.append((ms[h], acc * alpha + _dot(vt, p)))
        return tuple(out)

    init = tuple((jnp.full((1, tq), NEG_BIG, F32), jnp.zeros((DHB + ONES_ROWS, tq), F32))
                 for _ in range(HB))
    carry = lax.fori_loop(0, i // 2, lambda jj, c: step(2 * jj + 1, step(2 * jj, c, False), False), init)

    def finish(carry):
        o_t = jnp.concatenate([acc[:DHB] / acc[DHB:DHB + 1] for _, acc in carry], axis=0)
        o_ref[...] = o_t.T.astype(BF16)

    @pl.when(i % 2 == 0)
    def _():
        finish(step(i, carry, True))

    @pl.when(i % 2 == 1)
    def _():
        finish(step(i, step(i - 1, carry, False), True))


def _attn_prompt(qs, kaug, vt, bsz, S, tq):
    nq = S // tq
    once = pl.Buffered(1)
    return pl.pallas_call(
        functools.partial(_attn_kernel, tq=tq),
        grid=(bsz, nq),
        in_specs=[pl.BlockSpec((tq, DB), lambda b, i: (b * nq + i, 0)),
                  pl.BlockSpec((S, KAUG_WIDTH), lambda b, i: (b, 0), pipeline_mode=once),
                  pl.BlockSpec((nq, DB, tq), lambda b, i: (b, 0, 0), pipeline_mode=once)],
        out_specs=pl.BlockSpec((tq, DB), lambda b, i: (b * nq + i, 0)),
        out_shape=jax.ShapeDtypeStruct((bsz * S, DB), BF16),
        compiler_params=pltpu.CompilerParams(dimension_semantics=("parallel", "arbitrary"),
                                             vmem_limit_bytes=VMEM_LIMIT),
        name="attn_prompt",
    )(qs, kaug, vt)


def _attn_cached_kernel(q_ref, kc_ref, vc_ref, kn_ref, vn_ref, bhi_ref, bmid_ref, blo_ref, o_ref, *, L, P):
    q = q_ref[0]
    lane_head = lax.broadcasted_iota(jnp.int32, (L, DB), 1) // DHB
    q_bd = jnp.concatenate([jnp.where(lane_head == h, q, jnp.zeros_like(q)) for h in range(HB)], axis=0)
    kc = kc_ref[...].astype(BF16)
    vc = vc_ref[...].astype(BF16)
    zpad = jnp.zeros((LANES - L, DB), BF16)
    kn = jnp.concatenate([kn_ref[0].astype(BF16), zpad], axis=0)
    vn = jnp.concatenate([vn_ref[0].astype(BF16), zpad], axis=0)
    bias = (bhi_ref[0].astype(F32) + bmid_ref[0].astype(F32)) + blo_ref[0].astype(F32)

    s_c = _dot(q_bd, kc)
    s_n = _dot_nt(q_bd, kn)
    tt = lax.broadcasted_iota(jnp.int32, (L, LANES), 0)
    uu = lax.broadcasted_iota(jnp.int32, (L, LANES), 1)
    causal = uu <= tt
    pcs, pns, ls = [], [], []
    for h in range(HB):
        sc = s_c[h * L:(h + 1) * L, :] + bias[h:h + 1, :P]
        sn = jnp.where(causal, s_n[h * L:(h + 1) * L, :] + bias[h:h + 1, P:], NEG_BIG)
        m = jnp.maximum(jnp.max(sc, axis=1, keepdims=True), jnp.max(sn, axis=1, keepdims=True))
        pc = jnp.exp2(sc - m)
        pn = jnp.exp2(sn - m)
        ls.append(jnp.sum(pc, axis=1, keepdims=True) + jnp.sum(pn, axis=1, keepdims=True))
        pcs.append(pc.astype(BF16))
        pns.append(pn.astype(BF16))
    o_all = _dot_nt(jnp.concatenate(pcs, axis=0), vc) + _dot(jnp.concatenate(pns, axis=0), vn)
    out = jnp.zeros((L, DB), F32)
    for h in range(HB):
        out = out + jnp.where(lane_head == h, o_all[h * L:(h + 1) * L, :] / ls[h], 0.0)
    o_ref[0] = out.astype(BF16)


def _attn_cached(qs, k_cache_t, v_cache_t, layer, k_new, v_new, bhi, bmid, blo):
    bsz, L, _ = qs.shape
    P = k_cache_t.shape[-1]
    seq = lambda r, w: pl.BlockSpec((1, r, w), lambda b: (b, 0, 0))
    cache = pl.BlockSpec((None, None, DB, P), lambda b: (layer, b, 0, 0))
    return pl.pallas_call(
        functools.partial(_attn_cached_kernel, L=L, P=P),
        grid=(bsz,),
        in_specs=[seq(L, DB), cache, cache, seq(L, DB), seq(L, DB),
                  seq(HB, P + LANES), seq(HB, P + LANES), seq(HB, P + LANES)],
        out_specs=seq(L, DB),
        out_shape=jax.ShapeDtypeStruct((bsz, L, DB), BF16),
        compiler_params=pltpu.CompilerParams(dimension_semantics=("parallel",),
                                             vmem_limit_bytes=VMEM_LIMIT),
        name="attn_cached",
    )(qs, k_cache_t, v_cache_t, k_new, v_new, bhi, bmid, blo)


def _hgrn_exponent_matrix(C):
    nl = int(math.log2(C))
    rows = []
    idx = np.arange(C)
    for lv in range(nl):
        sz = 1 << lv
        m = np.zeros((C, C), np.float32)
        for t in range(C):
            bnd = ((t >> (lv + 1)) << (lv + 1)) + sz - 1
            if (t >> lv) & 1:
                m[t, (idx > bnd) & (idx <= t)] = 1.0
            else:
                m[t, (idx > t) & (idx <= bnd)] = 1.0
        rows.append(m)
    rows.append((idx[None, :] <= idx[:, None]).astype(np.float32))
    rows.append((idx[None, :] > idx[:, None]).astype(np.float32))
    p = np.concatenate(rows, axis=0)
    return np.concatenate([p, p], axis=1)


def _hgrn_kernel(p_ref, qh_ref, lf_ref, kk_ref, vv_ref, og_ref, s0_ref, g_ref, hg_ref, sout_ref, st_ref,
                 *, C, T):
    t = pl.program_id(1)
    nl = int(math.log2(C))

    @pl.when(t == 0)
    def _():
        for h in range(HC):
            st_ref[h] = s0_ref[0, h].T

    row = lax.broadcasted_iota(jnp.int32, (C, C), 0)
    col = lax.broadcasted_iota(jnp.int32, (C, C), 1)
    diff = row ^ col
    masks = [(jnp.right_shift(diff, lv) == 1) & ((jnp.right_shift(row, lv) & 1) == 1) for lv in range(nl)]
    diag = row == col
    g = g_ref[...]
    pmat = p_ref[...]

    pre = {}
    for c0 in range(0, T, C):
        lf = LOG2E * lf_ref[0, c0:c0 + C, :]
        hi = lf.astype(BF16)
        lo = (lf - hi.astype(F32)).astype(BF16)
        e_all = jnp.exp2(_dot(pmat, jnp.concatenate([hi, lo], axis=0)))
        for h in range(HC):
            hs = slice(h * DK, (h + 1) * DK)
            qb = qh_ref[0, c0:c0 + C, hs]
            kb = kk_ref[0, c0:c0 + C, hs]
            q = qb.astype(F32)
            k = kb.astype(F32)
            k_t = k.T
            a = jnp.where(diag, _dot(qb, k_t.astype(BF16)), 0.0)
            for lv in range(nl):
                e = e_all[lv * C:(lv + 1) * C, hs]
                a = jnp.where(masks[lv], _dot((q * e).astype(BF16), (k_t * e.T).astype(BF16)), a)
            e_q = e_all[nl * C:(nl + 1) * C, hs]
            e_k = e_all[(nl + 1) * C:(nl + 2) * C, hs]
            pre[c0, h] = (a.astype(BF16), (q * e_q).astype(BF16), (k * e_k).astype(BF16), e_q[C - 1:C, :])

    for h in range(HC):
        hs = slice(h * DK, (h + 1) * DK)
        st = st_ref[h]
        for c0 in range(0, T, C):
            a, q_dec, k_dec, chunk_dec = pre[c0, h]
            v = vv_ref[0, c0:c0 + C, hs]
            o = _dot(a, v) + _dot_nt(q_dec, st.astype(BF16))
            st = st * chunk_dec + _dot_tn(v, k_dec)
            o = o * lax.rsqrt(jnp.mean(o * o, axis=-1, keepdims=True) + EPS) * g
            hg_ref[0, c0:c0 + C, hs] = (o * og_ref[0, c0:c0 + C, hs].astype(F32)).astype(BF16)
        st_ref[h] = st

    @pl.when(t == pl.num_programs(1) - 1)
    def _():
        for h in range(HC):
            sout_ref[0, h] = st_ref[h].T


def _hgrn(qh, lfc, kk, vv, og, s0, g, layer, C, T):
    bsz, L, _ = qh.shape
    pmat = jnp.asarray(_hgrn_exponent_matrix(C), BF16)
    tile = pl.BlockSpec((1, T, DC), lambda b, t: (b, t, 0))
    state = pl.BlockSpec((1, HC, DK, DV), lambda b, t: (b, 0, 0, 0))
    return pl.pallas_call(
        functools.partial(_hgrn_kernel, C=C, T=T),
        grid=(bsz, L // T),
        in_specs=[_const_spec(pmat.shape), tile, tile, tile, tile, tile, state, _layer_spec((1, DV), layer)],
        out_specs=(tile, state),
        out_shape=(jax.ShapeDtypeStruct((bsz, L, DC), BF16),
                   jax.ShapeDtypeStruct((bsz, HC, DK, DV), F32)),
        scratch_shapes=[pltpu.VMEM((HC, DV, DK), F32)],
        compiler_params=pltpu.CompilerParams(dimension_semantics=("parallel", "arbitrary")),
        name="hgrn",
    )(pmat, qh, lfc, kk, vv, og, s0, g)


FF_CHUNK = 1024


def _merge_ffn_kernel(x_ref, fa_ref, fb_ref, fc_ref, g1_ref, wg_ref, wa_ref, wb_ref, wc_ref, wo_ref,
                      g2_ref, wup_ref, wdn_ref, gf_ref, o_ref, *, final):
    x = x_ref[...]
    h1 = _rms(x, g1_ref[...]).astype(BF16)
    m = None
    for i, (f_ref, w_ref) in enumerate(((fa_ref, wa_ref), (fb_ref, wb_ref), (fc_ref, wc_ref))):
        gate = _sigmoid(_dot(h1, wg_ref[:, i * D_MODEL:(i + 1) * D_MODEL]))
        y = gate * _dot(f_ref[...], w_ref[...])
        m = y if m is None else m + y
    x = x + _dot(m.astype(BF16), wo_ref[...])
    h2 = _rms(x, g2_ref[...]).astype(BF16)
    acc = None
    for c in range(0, D_FF, FF_CHUNK):
        up = jnp.maximum(_dot(h2, wup_ref[:, c:c + FF_CHUNK]), 0.0)
        d = _dot((up * up).astype(BF16), wdn_ref[c:c + FF_CHUNK, :])
        acc = d if acc is None else acc + d
    x = x + acc
    if final:
        x = _rms(x, gf_ref[...])
    o_ref[...] = x


def _merge_ffn(x, fa, fb, fc, w, layer, final_g, tm, final):
    n = x.shape[0]
    row = lambda w_: pl.BlockSpec((tm, w_), lambda i: (i, 0))
    lw = lambda *shape: _layer_spec(shape, layer)
    return pl.pallas_call(
        functools.partial(_merge_ffn_kernel, final=final),
        grid=(n // tm,),
        in_specs=[row(D_MODEL), row(CA), row(DB), row(DC),
                  lw(1, D_MODEL), lw(D_MODEL, 3 * D_MODEL),
                  lw(CA, D_MODEL), lw(DB, D_MODEL), lw(DC, D_MODEL),
                  lw(D_MODEL, D_MODEL), lw(1, D_MODEL),
                  lw(D_MODEL, D_FF), lw(D_FF, D_MODEL), _const_spec((1, D_MODEL))],
        out_specs=row(D_MODEL),
        out_shape=jax.ShapeDtypeStruct((n, D_MODEL), F32),
        compiler_params=pltpu.CompilerParams(dimension_semantics=("parallel",),
                                             vmem_limit_bytes=VMEM_LIMIT),
        name="merge_ffn",
    )(x, fa, fb, fc, w["g1"], w["w_gate"], w["w_a_out"], w["w_b_out"], w["w_c_out"], w["w_o"],
      w["g2"], w["w_up"], w["w_down"], final_g)


def _stacked_weights(p, lbs):
    w_in = p["w_in"]
    cols = lambda a, b: w_in[:, :, a:b]
    w_a = jnp.concatenate([cols(O_AV, O_QB), cols(O_QB, O_FB), cols(O_QC, O_GA),
                           jnp.pad(cols(O_FB, O_QC), ((0, 0), (0, 0), (0, LANES - HB)))], axis=2).astype(BF16)
    row = lambda v: v[:, None, :].astype(F32)
    lbrows = jnp.concatenate([jnp.log(lbs)[:, None], jnp.log1p(-lbs)[:, None], (1.0 - lbs)[:, None],
                              jnp.zeros((lbs.shape[0], 5, HC * DK), F32)], axis=1)
    return dict(
        w_a=w_a, g1=row(p["norm1_g"]),
        fbias=row(jnp.pad(p["fox_bf"], ((0, 0), (0, LANES - HB)))),
        lbrows=lbrows,
        conv_w=jnp.pad(p["conv_w"], ((0, 0), (0, HIST_ROWS - CONV_W), (0, 0))),
        conv_b=row(p["conv_b"]), ln_g=row(p["conv_ln_g"]), ln_b=row(p["conv_ln_b"]),
        hn_g=row(p["hgrn_norm_g"]),
        w_gate=cols(O_GA, O_END).astype(BF16),
        w_a_out=p["w_a_out"].astype(BF16), w_b_out=p["w_b_out"].astype(BF16),
        w_c_out=p["w_c_out"].astype(BF16), w_o=p["w_o"].astype(BF16),
        g2=row(p["norm2_g"]), w_up=p["w_up"].astype(BF16), w_down=p["w_down"].astype(BF16),
    )


def _pad_hist(h):
    return jnp.pad(h, ((0, 0), (HIST_PAD, 0), (0, 0)))


def _trunk_layer(x, w, layer, depth, stacks, final_g, final, conv_hist, s0, cache,
                 *, tm_a, tm_d, tl, tq, hg_tile):
    bsz, L, _ = x.shape
    n = bsz * L
    xf = x.reshape(n, D_MODEL)
    u, qs, k_all, kaug, v_all, vt, lf_all, qh, lfc, kk, vv, og = _in_proj(
        xf, w, layer, depth, stacks, tm_a, max(L // tm_a, 1), feature_major=cache is None)
    seq = lambda a: a.reshape(bsz, L, a.shape[-1])

    feat, conv_new = _conv(seq(u), _pad_hist(conv_hist), w, layer, tl)

    if cache is None:
        o_b = _attn_prompt(qs, kaug, vt, bsz, L, tq)
    else:
        k_cache_t, v_cache_t, lf_cache = cache
        P = k_cache_t.shape[-1]
        lf_t = seq(lf_all[layer]).transpose(0, 2, 1)
        lf_cat = jnp.concatenate([lf_cache.transpose(0, 2, 1), lf_t,
                                  jnp.zeros((bsz, HB, LANES - L), F32)], axis=-1)
        hi, mid, lo = _cumsum_bias(lf_cat, pivot=P - 1)
        o_b = _attn_cached(seq(qs), k_cache_t, v_cache_t, layer,
                           seq(k_all[layer]), seq(v_all[layer]), hi, mid, lo).reshape(n, DB)

    C = min(HGRN_CHUNK, L)
    hg, s_new = _hgrn(seq(qh), seq(lfc), seq(kk), seq(vv), seq(og), s0, w["hn_g"], layer, C, hg_tile)

    x_new = _merge_ffn(xf, feat.reshape(n, CA), o_b, hg.reshape(n, DC), w, layer, final_g, tm_d, final)
    return x_new.reshape(bsz, L, D_MODEL), (k_all, v_all, lf_all), conv_new[:, HIST_PAD:], s_new


def _lower_bounds(p):
    s = jax.nn.softmax(p.astype(F32), axis=0)
    return jnp.maximum(jnp.cumsum(s, axis=0) - s[0], 0.0)


def kernel(x_prompt, x_sample, cache_fox_k, cache_fox_v, cache_fox_logf, state_conv, state_hgrn,
           norm1_g, w_in, conv_w, conv_b, conv_ln_g, conv_ln_b, w_a_out, fox_bf, w_b_out,
           hgrn_lb_param, hgrn_norm_g, w_c_out, w_o, norm2_g, w_up, w_down, final_g):
    p = dict(norm1_g=norm1_g, w_in=w_in, conv_w=conv_w, conv_b=conv_b, conv_ln_g=conv_ln_g,
             conv_ln_b=conv_ln_b, w_a_out=w_a_out, fox_bf=fox_bf, w_b_out=w_b_out,
             hgrn_norm_g=hgrn_norm_g, w_c_out=w_c_out, w_o=w_o, norm2_g=norm2_g, w_up=w_up, w_down=w_down)
    depth = w_in.shape[0]
    lbs = _lower_bounds(hgrn_lb_param)
    fg = final_g.reshape(1, D_MODEL).astype(F32)
    xp, xs = x_prompt, x_sample
    bp, sp, _ = xp.shape
    bs, ls, _ = xs.shape
    zero_hist = jnp.zeros((bp, CONV_W - 1, CA), F32)
    zero_s = jnp.zeros((bp, HC, DK, DV), F32)
    past = cache_fox_k.shape[2]
    k_cache_t = cache_fox_k.transpose(0, 1, 3, 4, 2).reshape(depth, bs, DB, past)
    v_cache_t = cache_fox_v.transpose(0, 1, 3, 4, 2).reshape(depth, bs, DB, past)
    w = _stacked_weights(p, lbs)
    stacks_p = stacks_s = None
    conv_p, conv_s, hgrn_p, hgrn_s = [], [], [], []
    for l in range(depth):
        final = l == depth - 1
        xp, stacks_p, cp, hp = _trunk_layer(
            xp, w, l, depth, stacks_p, fg, final, zero_hist, zero_s, None,
            tm_a=min(512, bp * sp), tm_d=min(512, bp * sp), tl=min(512, sp),
            tq=min(512, sp), hg_tile=min(512, sp))
        xs, stacks_s, cs, hs = _trunk_layer(
            xs, w, l, depth, stacks_s, fg, final, state_conv[l], state_hgrn[l],
            (k_cache_t, v_cache_t, cache_fox_logf[l]),
            tm_a=min(512, bs * ls), tm_d=min(256, bs * ls), tl=ls, tq=None, hg_tile=ls)
        conv_p.append(cp); conv_s.append(cs); hgrn_p.append(hp); hgrn_s.append(hs)
    kp, vp, lfp = stacks_p
    ks, vs, lfs = stacks_s
    heads_last = lambda a: a.reshape(depth, bp, HB, DHB, sp).transpose(0, 1, 4, 2, 3)
    return (xp, xs,
            heads_last(kp), heads_last(vp),
            lfp.reshape(depth, bp, sp, HB), jnp.stack(conv_p), jnp.stack(hgrn_p),
            ks.reshape(depth, bs, ls, HB, DHB), vs.reshape(depth, bs, ls, HB, DHB),
            lfs.reshape(depth, bs, ls, HB), jnp.stack(conv_s), jnp.stack(hgrn_s))
```

```python
import functools
import math

import numpy as np
import jax
import jax.numpy as jnp
from jax import lax
from jax.experimental import pallas as pl
from jax.experimental.pallas import tpu as pltpu

D_MODEL = 1024
CONV_W = 31
CA = 512
HB = 8
DHB = 64
DB = HB * DHB
HC = 4
DK = 128
DV = 128
DC = HC * DV
D_FF = 4 * D_MODEL
EPS = 1e-6
HGRN_CHUNK = 128

LANES = 128
SUBLANES = 8
VMEM_LIMIT = 56 * 1024 * 1024
NEG_BIG = -1e30
LOG2E = 1.4426950408889634

F32 = jnp.float32
BF16 = jnp.bfloat16

_OFF = np.cumsum([0, CA, CA, DB, DB, DB, HB, HC * DK, HC * DK, DC, DC, D_MODEL, D_MODEL, D_MODEL])
(O_AV, O_AG, O_QB, O_KB, O_VB, O_FB, O_QC, O_FC, O_IC, O_OC, O_GA, O_GB, O_GC, O_END) = [int(v) for v in _OFF]


def _dot(a, b):
    return jnp.dot(a, b, preferred_element_type=F32)


def _dot_nt(a, b):
    return lax.dot_general(a, b, (((1,), (1,)), ((), ())), preferred_element_type=F32)


def _dot_tn(a, b):
    return lax.dot_general(a, b, (((0,), (0,)), ((), ())), preferred_element_type=F32)


def _sigmoid(x):
    return 1.0 / (1.0 + jnp.exp(-x))


def _silu(x):
    return x * _sigmoid(x)


def _rms(x, g):
    return x * lax.rsqrt(jnp.mean(x * x, axis=-1, keepdims=True) + EPS) * g


def _split3(x):
    hi = x.astype(BF16)
    r = x - hi.astype(F32)
    mid = r.astype(BF16)
    lo = (r - mid.astype(F32)).astype(BF16)
    return hi, mid, lo


def _const_spec(shape):
    nd = len(shape)
    return pl.BlockSpec(shape, lambda *_: (0,) * nd, pipeline_mode=pl.Buffered(1))


def _layer_spec(shape, layer):
    nd = len(shape)
    return pl.BlockSpec((None,) + tuple(shape), lambda *_: (layer,) + (0,) * nd, pipeline_mode=pl.Buffered(1))


SEG = 512
assert CA == DB == HC * DK == DC == SEG
A_WIDTH = 9 * SEG + LANES
PAIR_LANES = 2 * LANES
KAUG_WIDTH = (HB // 2) * PAIR_LANES


def _bias_placement_matrix():
    m = np.zeros((3 * LANES, (HB // 2) * LANES), np.float32)
    for head in range(HB):
        pair, e = divmod(head, 2)
        for part in range(3):
            m[part * LANES + head, pair * LANES + 3 * e + part] = 1.0
    return m


def _in_proj_kernel(x_ref, g1_ref, w_ref, fbias_ref, lb_ref, tri_ref, place_ref, *rest,
                    tiles_per_seq, n_alias, feature_major):
    (u_ref, qs_ref, k_ref, kaug_ref, v_ref, vt_ref, lf_ref,
     qh_ref, lfc_ref, kk_ref, vv_ref, og_ref, carry_ref) = rest[n_alias:]
    i = pl.program_id(0)

    @pl.when(i % tiles_per_seq == 0)
    def _():
        carry_ref[...] = jnp.zeros_like(carry_ref)

    x = x_ref[...]
    h = _rms(x, g1_ref[...]).astype(BF16)

    def seg(i, width=SEG):
        return _dot(h, w_ref[:, i * SEG:i * SEG + width])

    f = seg(9, LANES) + fbias_ref[...]
    lf = jnp.minimum(f, 0.0) - jnp.log(1.0 + jnp.exp(-jnp.abs(f)))
    lf_ref[...] = lf[:, :HB]

    hi, mid, lo = _split3(lf)
    c3 = _dot(tri_ref[...], jnp.concatenate([hi, mid, lo], axis=1))
    c = (c3[:, :LANES] + c3[:, LANES:2 * LANES]) + c3[:, 2 * LANES:] + carry_ref[0:1, :]
    carry_ref[0:1, :] = c[c.shape[0] - 1:, :]
    bh, bm, bl = _split3(-LOG2E * c)
    placed = _dot(jnp.concatenate([bh, bm, bl], axis=1), place_ref[...]).astype(BF16)

    z = seg(6)
    log_lb = lb_ref[0:1, :]
    log1m_lb = lb_ref[1:2, :]
    one_m_lb = lb_ref[2:3, :]
    e = jnp.exp(-jnp.abs(z))
    ls = jnp.minimum(z, 0.0) - jnp.log(1.0 + e)
    b = log1m_lb + ls
    mx = jnp.maximum(log_lb, b)
    lfc_ref[...] = mx + jnp.log(1.0 + jnp.exp(-jnp.abs(log_lb - b)))
    r = 1.0 / (1.0 + e)
    kk_ref[...] = (one_m_lb * jnp.where(z >= 0.0, e * r, r)).astype(BF16)

    u_ref[...] = seg(0) * _sigmoid(seg(1))
    qs_ref[...] = (seg(2) * (LOG2E * DHB ** -0.5)).astype(BF16)
    k = seg(3)
    for p in range(HB // 2):
        kaug_ref[:, 2 * p * LANES:(2 * p + 1) * LANES] = k[:, p * LANES:(p + 1) * LANES].astype(BF16)
        kaug_ref[:, (2 * p + 1) * LANES:(2 * p + 2) * LANES] = placed[:, p * LANES:(p + 1) * LANES]
    v = seg(4)
    v_t = v.T
    vt_ref[...] = v_t.astype(BF16)
    k_ref[...] = k.T if feature_major else k
    v_ref[...] = v_t if feature_major else v
    qh_ref[...] = _silu(seg(5)).astype(BF16)
    vv_ref[...] = seg(7).astype(BF16)
    og_ref[...] = _silu(seg(8)).astype(BF16)


def _in_proj(x, w, layer, depth, stacks, tm, tiles_per_seq, feature_major):
    n = x.shape[0]
    tok = lambda w_, dt: jax.ShapeDtypeStruct((n, w_), dt)
    row = lambda w_: pl.BlockSpec((tm, w_), lambda i: (i, 0))
    slab = lambda w_: pl.BlockSpec((None, tm, w_), lambda i: (layer, i, 0))
    stk = lambda w_: jax.ShapeDtypeStruct((depth, n, w_), F32)
    if feature_major:
        seq_len = tiles_per_seq * tm
        kv_stk = jax.ShapeDtypeStruct((depth, n // seq_len, DB, seq_len), F32)
        kv_slab = pl.BlockSpec((None, None, DB, tm),
                               lambda i: (layer, i // tiles_per_seq, 0, i % tiles_per_seq))
    else:
        kv_stk, kv_slab = stk(DB), slab(DB)
    tri = jnp.asarray(np.tril(np.ones((tm, tm), np.float32)), BF16)
    place = jnp.asarray(_bias_placement_matrix(), BF16)
    out_shape = (tok(CA, F32), tok(DB, BF16), kv_stk, tok(KAUG_WIDTH, BF16), kv_stk,
                 jax.ShapeDtypeStruct((n // tm, DB, tm), BF16), stk(HB),
                 tok(DC, BF16), tok(DC, F32), tok(DC, BF16), tok(DC, BF16), tok(DC, BF16))
    out_specs = (row(CA), row(DB), kv_slab, row(KAUG_WIDTH), kv_slab,
                 pl.BlockSpec((None, DB, tm), lambda i: (i, 0, 0)), slab(HB),
                 row(DC), row(DC), row(DC), row(DC), row(DC))
    in_specs = [row(D_MODEL), _layer_spec((1, D_MODEL), layer), _layer_spec((D_MODEL, A_WIDTH), layer),
                _layer_spec((1, LANES), layer), _layer_spec((SUBLANES, DC), layer), _const_spec((tm, tm)),
                _const_spec(place.shape)]
    args = [x, w["g1"], w["w_a"], w["fbias"], w["lbrows"], tri, place]
    aliases = {}
    if stacks is not None:
        aliases = {len(args) + a: out_idx for a, out_idx in enumerate((2, 4, 6))}
        in_specs += [pl.BlockSpec(memory_space=pl.ANY)] * len(stacks)
        args += list(stacks)
    return pl.pallas_call(
        functools.partial(_in_proj_kernel, tiles_per_seq=tiles_per_seq, n_alias=len(aliases),
                          feature_major=feature_major),
        grid=(n // tm,),
        in_specs=in_specs,
        out_specs=out_specs,
        out_shape=out_shape,
        input_output_aliases=aliases,
        scratch_shapes=[pltpu.VMEM((8, LANES), F32)],
        compiler_params=pltpu.CompilerParams(dimension_semantics=("arbitrary",),
                                             vmem_limit_bytes=VMEM_LIMIT),
        name="in_proj",
    )(*args)


def _cumsum_kernel(lf_ref, hi_ref, mid_ref, lo_ref, *, n, pivot):
    x = lf_ref[...]
    lane = lax.broadcasted_iota(jnp.int32, x.shape, 1)
    s = 1
    while s < n:
        x = x + jnp.where(lane >= s, pltpu.roll(x, s, 1), 0.0)
        s *= 2
    piv = jnp.sum(jnp.where(lane == pivot, x, 0.0), axis=1, keepdims=True)
    hi, mid, lo = _split3(LOG2E * (piv - x))
    hi_ref[...] = hi
    mid_ref[...] = mid
    lo_ref[...] = lo


def _cumsum_bias(lf_t, pivot):
    bsz, hh, n = lf_t.shape
    spec = pl.BlockSpec((bsz * hh, n), lambda i: (0, 0))
    sds = jax.ShapeDtypeStruct((bsz * hh, n), BF16)
    outs = pl.pallas_call(
        functools.partial(_cumsum_kernel, n=n, pivot=pivot),
        grid=(1,),
        in_specs=[spec],
        out_specs=(spec, spec, spec),
        out_shape=(sds, sds, sds),
        name="cumsum_bias",
    )(lf_t.reshape(bsz * hh, n))
    return tuple(o.reshape(bsz, hh, n) for o in outs)


HIST_ROWS = 32
HIST_PAD = HIST_ROWS - (CONV_W - 1)


def _conv_kernel(u_ref, hist_ref, w_ref, cb_ref, g_ref, b_ref, feat_ref, new_ref, buf_ref, *, tl, rc):
    t = pl.program_id(1)

    @pl.when(t == 0)
    def _():
        buf_ref[0:HIST_ROWS, :] = hist_ref[0]

    @pl.when(t > 0)
    def _():
        buf_ref[0:HIST_ROWS, :] = buf_ref[tl:tl + HIST_ROWS, :]

    buf_ref[HIST_ROWS:HIST_ROWS + tl, :] = u_ref[0]
    new_ref[0] = buf_ref[tl:tl + HIST_ROWS, :]

    cb = cb_ref[...]
    g = g_ref[...]
    b = b_ref[...]
    for r0 in range(0, tl, rc):
        y = cb
        for r in range(SUBLANES):
            z = None
            for a in range((HIST_ROWS + SUBLANES) // SUBLANES):
                j = SUBLANES * a + r - HIST_PAD
                if 0 <= j < CONV_W:
                    nrows = rc + (SUBLANES if r else 0)
                    term = w_ref[j:j + 1, :] * buf_ref[r0 + SUBLANES * a:r0 + SUBLANES * a + nrows, :]
                    z = term if z is None else z + term
            y = y + z[r:r + rc]
        mu = jnp.mean(y, axis=-1, keepdims=True)
        yc = y - mu
        yn = yc * lax.rsqrt(jnp.mean(yc * yc, axis=-1, keepdims=True) + EPS) * g + b
        feat_ref[0, r0:r0 + rc, :] = _silu(yn).astype(BF16)


def _conv(u, hist, w, layer, tl):
    bsz, L, _ = u.shape
    rc = min(tl, 64)
    vec = _layer_spec((1, CA), layer)
    return pl.pallas_call(
        functools.partial(_conv_kernel, tl=tl, rc=rc),
        grid=(bsz, L // tl),
        in_specs=[pl.BlockSpec((1, tl, CA), lambda i, t: (i, t, 0)),
                  pl.BlockSpec((1, HIST_ROWS, CA), lambda i, t: (i, 0, 0)),
                  _layer_spec((HIST_ROWS, CA), layer), vec, vec, vec],
        out_specs=(pl.BlockSpec((1, tl, CA), lambda i, t: (i, t, 0)),
                   pl.BlockSpec((1, HIST_ROWS, CA), lambda i, t: (i, 0, 0))),
        out_shape=(jax.ShapeDtypeStruct((bsz, L, CA), BF16),
                   jax.ShapeDtypeStruct((bsz, HIST_ROWS, CA), F32)),
        scratch_shapes=[pltpu.VMEM((tl + HIST_ROWS, CA), F32)],
        compiler_params=pltpu.CompilerParams(dimension_semantics=("parallel", "arbitrary")),
        name="conv",
    )(u, hist, w["conv_w"], w["conv_b"], w["ln_g"], w["ln_b"])


ONES_ROWS = 16


def _attn_kernel(q_ref, k_ref, vt_ref, o_ref, *, tq):
    i = pl.program_id(1)
    lane = lax.broadcasted_iota(jnp.int32, (tq, LANES), 1)
    qm = []
    for p in range(HB // 2):
        q = q_ref[:, p * LANES:(p + 1) * LANES]
        for e in range(2):
            qa = jnp.where(lane // DHB == e, q, jnp.zeros_like(q))
            pick = jnp.where((lane >= 3 * e) & (lane < 3 * e + 3), 1.0, 0.0).astype(BF16)
            qm.append(jnp.concatenate([qa, pick], axis=1))
    ones = jnp.ones((ONES_ROWS, tq), BF16)

    def step(j, carry, diag):
        rows = pl.ds(pl.multiple_of(j * tq, tq), tq)
        if diag:
            kk = lax.broadcasted_iota(jnp.int32, (tq, tq), 0)
            qq = lax.broadcasted_iota(jnp.int32, (tq, tq), 1)
            keep = kk <= qq
        ss = []
        for p in range(HB // 2):
            kblk = k_ref[rows, p * PAIR_LANES:(p + 1) * PAIR_LANES]
            for e in range(2):
                ss.append(_dot_nt(kblk, qm[2 * p + e]))
        if diag:
            ss = [jnp.where(keep, s, NEG_BIG) for s in ss]
        ms = [jnp.maximum(carry[h][0], jnp.max(ss[h], axis=0, keepdims=True)) for h in range(HB)]
        out = []
        for h in range(HB):
            m, acc = carry[h]
            alpha = jnp.exp2(m - ms[h])
            p = jnp.exp2(ss[h] - ms[h]).astype(BF16)
            vt = jnp.concatenate([vt_ref[j, h * DHB:(h + 1) * DHB, :], ones], axis=0)
            out.append((ms[h], acc * alpha + _dot(vt, p)))
        return tuple(out)

    init = tuple((jnp.full((1, tq), NEG_BIG, F32), jnp.zeros((DHB + ONES_ROWS, tq), F32))
                 for _ in range(HB))
    carry = lax.fori_loop(0, i // 2, lambda jj, c: step(2 * jj + 1, step(2 * jj, c, False), False), init)

    def finish(carry):
        o_t = jnp.concatenate([acc[:DHB] / acc[DHB:DHB + 1] for _, acc in carry], axis=0)
        o_ref[...] = o_t.T.astype(BF16)

    @pl.when(i % 2 == 0)
    def _():
        finish(step(i, carry, True))

    @pl.when(i % 2 == 1)
    def _():
        finish(step(i, step(i - 1, carry, False), True))


def _attn_prompt(qs, kaug, vt, bsz, S, tq):
    nq = S // tq
    once = pl.Buffered(1)
    return pl.pallas_call(
        functools.partial(_attn_kernel, tq=tq),
        grid=(bsz, nq),
        in_specs=[pl.BlockSpec((tq, DB), lambda b, i: (b * nq + i, 0)),
                  pl.BlockSpec((S, KAUG_WIDTH), lambda b, i: (b, 0), pipeline_mode=once),
                  pl.BlockSpec((nq, DB, tq), lambda b, i: (b, 0, 0), pipeline_mode=once)],
        out_specs=pl.BlockSpec((tq, DB), lambda b, i: (b * nq + i, 0)),
        out_shape=jax.ShapeDtypeStruct((bsz * S, DB), BF16),
        compiler_params=pltpu.CompilerParams(dimension_semantics=("parallel", "arbitrary"),
                                             vmem_limit_bytes=VMEM_LIMIT),
        name="attn_prompt",
    )(qs, kaug, vt)


def _attn_cached_kernel(q_ref, kc_ref, vc_ref, kn_ref, vn_ref, bhi_ref, bmid_ref, blo_ref, o_ref, *, L, P):
    q = q_ref[0]
    lane_head = lax.broadcasted_iota(jnp.int32, (L, DB), 1) // DHB
    q_bd = jnp.concatenate([jnp.where(lane_head == h, q, jnp.zeros_like(q)) for h in range(HB)], axis=0)
    kc = kc_ref[...].astype(BF16)
    vc = vc_ref[...].astype(BF16)
    zpad = jnp.zeros((LANES - L, DB), BF16)
    kn = jnp.concatenate([kn_ref[0].astype(BF16), zpad], axis=0)
    vn = jnp.concatenate([vn_ref[0].astype(BF16), zpad], axis=0)
    bias = (bhi_ref[0].astype(F32) + bmid_ref[0].astype(F32)) + blo_ref[0].astype(F32)

    s_c = _dot(q_bd, kc)
    s_n = _dot_nt(q_bd, kn)
    tt = lax.broadcasted_iota(jnp.int32, (L, LANES), 0)
    uu = lax.broadcasted_iota(jnp.int32, (L, LANES), 1)
    causal = uu <= tt
    pcs, pns, ls = [], [], []
    for h in range(HB):
        sc = s_c[h * L:(h + 1) * L, :] + bias[h:h + 1, :P]
        sn = jnp.where(causal, s_n[h * L:(h + 1) * L, :] + bias[h:h + 1, P:], NEG_BIG)
        m = jnp.maximum(jnp.max(sc, axis=1, keepdims=True), jnp.max(sn, axis=1, keepdims=True))
        pc = jnp.exp2(sc - m)
        pn = jnp.exp2(sn - m)
        ls.append(jnp.sum(pc, axis=1, keepdims=True) + jnp.sum(pn, axis=1, keepdims=True))
        pcs.append(pc.astype(BF16))
        pns.append(pn.astype(BF16))
    o_all = _dot_nt(jnp.concatenate(pcs, axis=0), vc) + _dot(jnp.concatenate(pns, axis=0), vn)
    out = jnp.zeros((L, DB), F32)
    for h in range(HB):
        out = out + jnp.where(lane_head == h, o_all[h * L:(h + 1) * L, :] / ls[h], 0.0)
    o_ref[0] = out.astype(BF16)


def _attn_cached(qs, k_cache_t, v_cache_t, layer, k_new, v_new, bhi, bmid, blo):
    bsz, L, _ = qs.shape
    P = k_cache_t.shape[-1]
    seq = lambda r, w: pl.BlockSpec((1, r, w), lambda b: (b, 0, 0))
    cache = pl.BlockSpec((None, None, DB, P), lambda b: (layer, b, 0, 0))
    return pl.pallas_call(
        functools.partial(_attn_cached_kernel, L=L, P=P),
        grid=(bsz,),
        in_specs=[seq(L, DB), cache, cache, seq(L, DB), seq(L, DB),
                  seq(HB, P + LANES), seq(HB, P + LANES), seq(HB, P + LANES)],
        out_specs=seq(L, DB),
        out_shape=jax.ShapeDtypeStruct((bsz, L, DB), BF16),
        compiler_params=pltpu.CompilerParams(dimension_semantics=("parallel",),
                                             vmem_limit_bytes=VMEM_LIMIT),
        name="attn_cached",
    )(qs, k_cache_t, v_cache_t, k_new, v_new, bhi, bmid, blo)


def _hgrn_exponent_matrix(C):
    nl = int(math.log2(C))
    rows = []
    idx = np.arange(C)
    for lv in range(nl):
        sz = 1 << lv
        m = np.zeros((C, C), np.float32)
        for t in range(C):
            bnd = ((t >> (lv + 1)) << (lv + 1)) + sz - 1
            if (t >> lv) & 1:
                m[t, (idx > bnd) & (idx <= t)] = 1.0
            else:
                m[t, (idx > t) & (idx <= bnd)] = 1.0
        rows.append(m)
    rows.append((idx[None, :] <= idx[:, None]).astype(np.float32))
    rows.append((idx[None, :] > idx[:, None]).astype(np.float32))
    p = np.concatenate(rows, axis=0)
    return np.concatenate([p, p], axis=1)


def _hgrn_kernel(p_ref, qh_ref, lf_ref, kk_ref, vv_ref, og_ref, s0_ref, g_ref, hg_ref, sout_ref, st_ref,
                 *, C, T):
    t = pl.program_id(1)
    nl = int(math.log2(C))

    @pl.when(t == 0)
    def _():
        for h in range(HC):
            st_ref[h] = s0_ref[0, h].T

    row = lax.broadcasted_iota(jnp.int32, (C, C), 0)
    col = lax.broadcasted_iota(jnp.int32, (C, C), 1)
    diff = row ^ col
    masks = [(jnp.right_shift(diff, lv) == 1) & ((jnp.right_shift(row, lv) & 1) == 1) for lv in range(nl)]
    diag = row == col
    g = g_ref[...]
    pmat = p_ref[...]

    pre = {}
    for c0 in range(0, T, C):
        lf = LOG2E * lf_ref[0, c0:c0 + C, :]
        hi = lf.astype(BF16)
        lo = (lf - hi.astype(F32)).astype(BF16)
        e_all = jnp.exp2(_dot(pmat, jnp.concatenate([hi, lo], axis=0)))
        for h in range(HC):
            hs = slice(h * DK, (h + 1) * DK)
            qb = qh_ref[0, c0:c0 + C, hs]
            kb = kk_ref[0, c0:c0 + C, hs]
            q = qb.astype(F32)
            k = kb.astype(F32)
            k_t = k.T
            a = jnp.where(diag, _dot(qb, k_t.astype(BF16)), 0.0)
            for lv in range(nl):
                e = e_all[lv * C:(lv + 1) * C, hs]
                a = jnp.where(masks[lv], _dot((q * e).astype(BF16), (k_t * e.T).astype(BF16)), a)
            e_q = e_all[nl * C:(nl + 1) * C, hs]
            e_k = e_all[(nl + 1) * C:(nl + 2) * C, hs]
            pre[c0, h] = (a.astype(BF16), (q * e_q).astype(BF16), (k * e_k).astype(BF16), e_q[C - 1:C, :])

    for h in range(HC):
        hs = slice(h * DK, (h + 1) * DK)
        st = st_ref[h]
        for c0 in range(0, T, C):
            a, q_dec, k_dec, chunk_dec = pre[c0, h]
            v = vv_ref[0, c0:c0 + C, hs]
            o = _dot(a, v) + _dot_nt(q_dec, st.astype(BF16))
            st = st * chunk_dec + _dot_tn(v, k_dec)
            o = o * lax.rsqrt(jnp.mean(o * o, axis=-1, keepdims=True) + EPS) * g
            hg_ref[0, c0:c0 + C, hs] = (o * og_ref[0, c0:c0 + C, hs].astype(F32)).astype(BF16)
        st_ref[h] = st

    @pl.when(t == pl.num_programs(1) - 1)
    def _():
        for h in range(HC):
            sout_ref[0, h] = st_ref[h].T


def _hgrn(qh, lfc, kk, vv, og, s0, g, layer, C, T):
    bsz, L, _ = qh.shape
    pmat = jnp.asarray(_hgrn_exponent_matrix(C), BF16)
    tile = pl.BlockSpec((1, T, DC), lambda b, t: (b, t, 0))
    state = pl.BlockSpec((1, HC, DK, DV), lambda b, t: (b, 0, 0, 0))
    return pl.pallas_call(
        functools.partial(_hgrn_kernel, C=C, T=T),
        grid=(bsz, L // T),
        in_specs=[_const_spec(pmat.shape), tile, tile, tile, tile, tile, state, _layer_spec((1, DV), layer)],
        out_specs=(tile, state),
        out_shape=(jax.ShapeDtypeStruct((bsz, L, DC), BF16),
                   jax.ShapeDtypeStruct((bsz, HC, DK, DV), F32)),
        scratch_shapes=[pltpu.VMEM((HC, DV, DK), F32)],
        compiler_params=pltpu.CompilerParams(dimension_semantics=("parallel", "arbitrary")),
        name="hgrn",
    )(pmat, qh, lfc, kk, vv, og, s0, g)


FF_CHUNK = 1024


def _merge_ffn_kernel(x_ref, fa_ref, fb_ref, fc_ref, g1_ref, wg_ref, wa_ref, wb_ref, wc_ref, wo_ref,
                      g2_ref, wup_ref, wdn_ref, gf_ref, o_ref, *, final):
    x = x_ref[...]
    h1 = _rms(x, g1_ref[...]).astype(BF16)
    m = None
    for i, (f_ref, w_ref) in enumerate(((fa_ref, wa_ref), (fb_ref, wb_ref), (fc_ref, wc_ref))):
        gate = _sigmoid(_dot(h1, wg_ref[:, i * D_MODEL:(i + 1) * D_MODEL]))
        y = gate * _dot(f_ref[...], w_ref[...])
        m = y if m is None else m + y
    x = x + _dot(m.astype(BF16), wo_ref[...])
    h2 = _rms(x, g2_ref[...]).astype(BF16)
    acc = None
    for c in range(0, D_FF, FF_CHUNK):
        up = jnp.maximum(_dot(h2, wup_ref[:, c:c + FF_CHUNK]), 0.0)
        d = _dot((up * up).astype(BF16), wdn_ref[c:c + FF_CHUNK, :])
        acc = d if acc is None else acc + d
    x = x + acc
    if final:
        x = _rms(x, gf_ref[...])
    o_ref[...] = x


def _merge_ffn(x, fa, fb, fc, w, layer, final_g, tm, final):
    n = x.shape[0]
    row = lambda w_: pl.BlockSpec((tm, w_), lambda i: (i, 0))
    lw = lambda *shape: _layer_spec(shape, layer)
    return pl.pallas_call(
        functools.partial(_merge_ffn_kernel, final=final),
        grid=(n // tm,),
        in_specs=[row(D_MODEL), row(CA), row(DB), row(DC),
                  lw(1, D_MODEL), lw(D_MODEL, 3 * D_MODEL),
                  lw(CA, D_MODEL), lw(DB, D_MODEL), lw(DC, D_MODEL),
                  lw(D_MODEL, D_MODEL), lw(1, D_MODEL),
                  lw(D_MODEL, D_FF), lw(D_FF, D_MODEL), _const_spec((1, D_MODEL))],
        out_specs=row(D_MODEL),
        out_shape=jax.ShapeDtypeStruct((n, D_MODEL), F32),
        compiler_params=pltpu.CompilerParams(dimension_semantics=("parallel",),
                                             vmem_limit_bytes=VMEM_LIMIT),
        name="merge_ffn",
    )(x, fa, fb, fc, w["g1"], w["w_gate"], w["w_a_out"], w["w_b_out"], w["w_c_out"], w["w_o"],
      w["g2"], w["w_up"], w["w_down"], final_g)


def _stacked_weights(p, lbs):
    w_in = p["w_in"]
    cols = lambda a, b: w_in[:, :, a:b]
    w_a = jnp.concatenate([cols(O_AV, O_QB), cols(O_QB, O_FB), cols(O_QC, O_GA),
                           jnp.pad(cols(O_FB, O_QC), ((0, 0), (0, 0), (0, LANES - HB)))], axis=2).astype(BF16)
    row = lambda v: v[:, None, :].astype(F32)
    lbrows = jnp.concatenate([jnp.log(lbs)[:, None], jnp.log1p(-lbs)[:, None], (1.0 - lbs)[:, None],
                              jnp.zeros((lbs.shape[0], 5, HC * DK), F32)], axis=1)
    return dict(
        w_a=w_a, g1=row(p["norm1_g"]),
        fbias=row(jnp.pad(p["fox_bf"], ((0, 0), (0, LANES - HB)))),
        lbrows=lbrows,
        conv_w=jnp.pad(p["conv_w"], ((0, 0), (0, HIST_ROWS - CONV_W), (0, 0))),
        conv_b=row(p["conv_b"]), ln_g=row(p["conv_ln_g"]), ln_b=row(p["conv_ln_b"]),
        hn_g=row(p["hgrn_norm_g"]),
        w_gate=cols(O_GA, O_END).astype(BF16),
        w_a_out=p["w_a_out"].astype(BF16), w_b_out=p["w_b_out"].astype(BF16),
        w_c_out=p["w_c_out"].astype(BF16), w_o=p["w_o"].astype(BF16),
        g2=row(p["norm2_g"]), w_up=p["w_up"].astype(BF16), w_down=p["w_down"].astype(BF16),
    )


def _pad_hist(h):
    return jnp.pad(h, ((0, 0), (HIST_PAD, 0), (0, 0)))


def _trunk_layer(x, w, layer, depth, stacks, final_g, final, conv_hist, s0, cache,
                 *, tm_a, tm_d, tl, tq, hg_tile):
    bsz, L, _ = x.shape
    n = bsz * L
    xf = x.reshape(n, D_MODEL)
    u, qs, k_all, kaug, v_all, vt, lf_all, qh, lfc, kk, vv, og = _in_proj(
        xf, w, layer, depth, stacks, tm_a, max(L // tm_a, 1), feature_major=cache is None)
    seq = lambda a: a.reshape(bsz, L, a.shape[-1])

    feat, conv_new = _conv(seq(u), _pad_hist(conv_hist), w, layer, tl)

    if cache is None:
        o_b = _attn_prompt(qs, kaug, vt, bsz, L, tq)
    else:
        k_cache_t, v_cache_t, lf_cache = cache
        P = k_cache_t.shape[-1]
        lf_t = seq(lf_all[layer]).transpose(0, 2, 1)
        lf_cat = jnp.concatenate([lf_cache.transpose(0, 2, 1), lf_t,
                                  jnp.zeros((bsz, HB, LANES - L), F32)], axis=-1)
        hi, mid, lo = _cumsum_bias(lf_cat, pivot=P - 1)
        o_b = _attn_cached(seq(qs), k_cache_t, v_cache_t, layer,
                           seq(k_all[layer]), seq(v_all[layer]), hi, mid, lo).reshape(n, DB)

    C = min(HGRN_CHUNK, L)
    hg, s_new = _hgrn(seq(qh), seq(lfc), seq(kk), seq(vv), seq(og), s0, w["hn_g"], layer, C, hg_tile)

    x_new = _merge_ffn(xf, feat.reshape(n, CA), o_b, hg.reshape(n, DC), w, layer, final_g, tm_d, final)
    return x_new.reshape(bsz, L, D_MODEL), (k_all, v_all, lf_all), conv_new[:, HIST_PAD:], s_new


def _lower_bounds(p):
    s = jax.nn.softmax(p.astype(F32), axis=0)
    return jnp.maximum(jnp.cumsum(s, axis=0) - s[0], 0.0)


TOKEN_TILE = 512
HGRN_TILE = 1024
SAMPLE_FFN_TILE = 256


def _tile_plan(bsz, seq, prompt):
    n = bsz * seq
    if prompt:
        t = min(TOKEN_TILE, seq)
        return dict(tm_a=t, tm_d=min(TOKEN_TILE, n), tl=t, tq=t, hg_tile=min(HGRN_TILE, seq))
    return dict(tm_a=min(TOKEN_TILE, n), tm_d=min(SAMPLE_FFN_TILE, n), tl=seq, tq=None, hg_tile=seq)


def kernel(x_prompt, x_sample, cache_fox_k, cache_fox_v, cache_fox_logf, state_conv, state_hgrn,
           norm1_g, w_in, conv_w, conv_b, conv_ln_g, conv_ln_b, w_a_out, fox_bf, w_b_out,
           hgrn_lb_param, hgrn_norm_g, w_c_out, w_o, norm2_g, w_up, w_down, final_g):
    p = dict(norm1_g=norm1_g, w_in=w_in, conv_w=conv_w, conv_b=conv_b, conv_ln_g=conv_ln_g,
             conv_ln_b=conv_ln_b, w_a_out=w_a_out, fox_bf=fox_bf, w_b_out=w_b_out,
             hgrn_norm_g=hgrn_norm_g, w_c_out=w_c_out, w_o=w_o, norm2_g=norm2_g, w_up=w_up, w_down=w_down)
    depth = w_in.shape[0]
    lbs = _lower_bounds(hgrn_lb_param)
    fg = final_g.reshape(1, D_MODEL).astype(F32)
    xp, xs = x_prompt, x_sample
    bp, sp, _ = xp.shape
    bs, ls, _ = xs.shape
    zero_hist = jnp.zeros((bp, CONV_W - 1, CA), F32)
    zero_s = jnp.zeros((bp, HC, DK, DV), F32)
    past = cache_fox_k.shape[2]
    k_cache_t = cache_fox_k.transpose(0, 1, 3, 4, 2).reshape(depth, bs, DB, past)
    v_cache_t = cache_fox_v.transpose(0, 1, 3, 4, 2).reshape(depth, bs, DB, past)
    w = _stacked_weights(p, lbs)
    stacks_p = stacks_s = None
    conv_p, conv_s, hgrn_p, hgrn_s = [], [], [], []
    for l in range(depth):
        final = l == depth - 1
        xp, stacks_p, cp, hp = _trunk_layer(
            xp, w, l, depth, stacks_p, fg, final, zero_hist, zero_s, None, **_tile_plan(bp, sp, True))
        xs, stacks_s, cs, hs = _trunk_layer(
            xs, w, l, depth, stacks_s, fg, final, state_conv[l], state_hgrn[l],
            (k_cache_t, v_cache_t, cache_fox_logf[l]), **_tile_plan(bs, ls, False))
        conv_p.append(cp); conv_s.append(cs); hgrn_p.append(hp); hgrn_s.append(hs)
    kp, vp, lfp = stacks_p
    ks, vs, lfs = stacks_s
    heads_last = lambda a: a.reshape(depth, bp, HB, DHB, sp).transpose(0, 1, 4, 2, 3)
    return (xp, xs,
            heads_last(kp), heads_last(vp),
            lfp.reshape(depth, bp, sp, HB), jnp.stack(conv_p), jnp.stack(hgrn_p),
            ks.reshape(depth, bs, ls, HB, DHB), vs.reshape(depth, bs, ls, HB, DHB),
            lfs.reshape(depth, bs, ls, HB), jnp.stack(conv_s), jnp.stack(hgrn_s))
```

```python
import functools
import math

import numpy as np
import jax
import jax.numpy as jnp
from jax import lax
from jax.experimental import pallas as pl
from jax.experimental.pallas import tpu as pltpu

D_MODEL = 1024
CONV_W = 31
CA = 512
HB = 8
DHB = 64
DB = HB * DHB
HC = 4
DK = 128
DV = 128
DC = HC * DV
D_FF = 4 * D_MODEL
EPS = 1e-6
HGRN_CHUNK = 128

LANES = 128
SUBLANES = 8
VMEM_LIMIT = 56 * 1024 * 1024
NEG_BIG = -1e30
LOG2E = 1.4426950408889634

F32 = jnp.float32
BF16 = jnp.bfloat16

_OFF = np.cumsum([0, CA, CA, DB, DB, DB, HB, HC * DK, HC * DK, DC, DC, D_MODEL, D_MODEL, D_MODEL])
(O_AV, O_AG, O_QB, O_KB, O_VB, O_FB, O_QC, O_FC, O_IC, O_OC, O_GA, O_GB, O_GC, O_END) = [int(v) for v in _OFF]


def _dot(a, b):
    return jnp.dot(a, b, preferred_element_type=F32)


def _dot_nt(a, b):
    return lax.dot_general(a, b, (((1,), (1,)), ((), ())), preferred_element_type=F32)


def _dot_tn(a, b):
    return lax.dot_general(a, b, (((0,), (0,)), ((), ())), preferred_element_type=F32)


def _sigmoid(x):
    return 1.0 / (1.0 + jnp.exp(-x))


def _silu(x):
    return x * _sigmoid(x)


def _rms(x, g):
    return x * lax.rsqrt(jnp.mean(x * x, axis=-1, keepdims=True) + EPS) * g


def _split3(x):
    hi = x.astype(BF16)
    r = x - hi.astype(F32)
    mid = r.astype(BF16)
    lo = (r - mid.astype(F32)).astype(BF16)
    return hi, mid, lo


def _const_spec(shape):
    nd = len(shape)
    return pl.BlockSpec(shape, lambda *_: (0,) * nd, pipeline_mode=pl.Buffered(1))


def _layer_spec(shape, layer):
    nd = len(shape)
    return pl.BlockSpec((None,) + tuple(shape), lambda *_: (layer,) + (0,) * nd, pipeline_mode=pl.Buffered(1))


SEG = 512
assert CA == DB == HC * DK == DC == SEG
A_WIDTH = 9 * SEG + LANES
PAIR_LANES = 2 * LANES
KAUG_WIDTH = (HB // 2) * PAIR_LANES


def _bias_placement_matrix():
    m = np.zeros((3 * LANES, (HB // 2) * LANES), np.float32)
    for head in range(HB):
        pair, e = divmod(head, 2)
        for part in range(3):
            m[part * LANES + head, pair * LANES + 3 * e + part] = 1.0
    return m


def _in_proj_kernel(x_ref, g1_ref, w_ref, fbias_ref, lb_ref, tri_ref, place_ref, *rest,
                    tiles_per_seq, n_alias, feature_major):
    (u_ref, qs_ref, k_ref, kaug_ref, v_ref, vt_ref, lf_ref,
     qh_ref, lfc_ref, kk_ref, vv_ref, og_ref, carry_ref) = rest[n_alias:]
    i = pl.program_id(0)

    @pl.when(i % tiles_per_seq == 0)
    def _():
        carry_ref[...] = jnp.zeros_like(carry_ref)

    x = x_ref[...]
    h = _rms(x, g1_ref[...]).astype(BF16)

    def seg(i, width=SEG):
        return _dot(h, w_ref[:, i * SEG:i * SEG + width])

    f = seg(9, LANES) + fbias_ref[...]
    lf = jnp.minimum(f, 0.0) - jnp.log(1.0 + jnp.exp(-jnp.abs(f)))
    lf_ref[...] = lf[:, :HB]

    hi, mid, lo = _split3(lf)
    c3 = _dot(tri_ref[...], jnp.concatenate([hi, mid, lo], axis=1))
    c = (c3[:, :LANES] + c3[:, LANES:2 * LANES]) + c3[:, 2 * LANES:] + carry_ref[0:1, :]
    carry_ref[0:1, :] = c[c.shape[0] - 1:, :]
    bh, bm, bl = _split3(-LOG2E * c)
    placed = _dot(jnp.concatenate([bh, bm, bl], axis=1), place_ref[...]).astype(BF16)

    z = seg(6)
    log_lb = lb_ref[0:1, :]
    log1m_lb = lb_ref[1:2, :]
    one_m_lb = lb_ref[2:3, :]
    e = jnp.exp(-jnp.abs(z))
    ls = jnp.minimum(z, 0.0) - jnp.log(1.0 + e)
    b = log1m_lb + ls
    mx = jnp.maximum(log_lb, b)
    lfc_ref[...] = mx + jnp.log(1.0 + jnp.exp(-jnp.abs(log_lb - b)))
    r = 1.0 / (1.0 + e)
    kk_ref[...] = (one_m_lb * jnp.where(z >= 0.0, e * r, r)).astype(BF16)

    u_ref[...] = seg(0) * _sigmoid(seg(1))
    qs_ref[...] = (seg(2) * (LOG2E * DHB ** -0.5)).astype(BF16)
    k = seg(3)
    for p in range(HB // 2):
        kaug_ref[:, 2 * p * LANES:(2 * p + 1) * LANES] = k[:, p * LANES:(p + 1) * LANES].astype(BF16)
        kaug_ref[:, (2 * p + 1) * LANES:(2 * p + 2) * LANES] = placed[:, p * LANES:(p + 1) * LANES]
    v = seg(4)
    v_t = v.T
    vt_ref[...] = v_t.astype(BF16)
    k_ref[...] = k.T if feature_major else k
    v_ref[...] = v_t if feature_major else v
    qh_ref[...] = _silu(seg(5)).astype(BF16)
    vv_ref[...] = seg(7).astype(BF16)
    og_ref[...] = _silu(seg(8)).astype(BF16)


def _in_proj(x, w, layer, depth, stacks, tm, tiles_per_seq, feature_major):
    n = x.shape[0]
    tok = lambda w_, dt: jax.ShapeDtypeStruct((n, w_), dt)
    row = lambda w_: pl.BlockSpec((tm, w_), lambda i: (i, 0))
    slab = lambda w_: pl.BlockSpec((None, tm, w_), lambda i: (layer, i, 0))
    stk = lambda w_: jax.ShapeDtypeStruct((depth, n, w_), F32)
    if feature_major:
        seq_len = tiles_per_seq * tm
        kv_stk = jax.ShapeDtypeStruct((depth, n // seq_len, DB, seq_len), F32)
        kv_slab = pl.BlockSpec((None, None, DB, tm),
                               lambda i: (layer, i // tiles_per_seq, 0, i % tiles_per_seq))
    else:
        kv_stk, kv_slab = stk(DB), slab(DB)
    tri = jnp.asarray(np.tril(np.ones((tm, tm), np.float32)), BF16)
    place = jnp.asarray(_bias_placement_matrix(), BF16)
    out_shape = (tok(CA, F32), tok(DB, BF16), kv_stk, tok(KAUG_WIDTH, BF16), kv_stk,
                 jax.ShapeDtypeStruct((n // tm, DB, tm), BF16), stk(HB),
                 tok(DC, BF16), tok(DC, F32), tok(DC, BF16), tok(DC, BF16), tok(DC, BF16))
    out_specs = (row(CA), row(DB), kv_slab, row(KAUG_WIDTH), kv_slab,
                 pl.BlockSpec((None, DB, tm), lambda i: (i, 0, 0)), slab(HB),
                 row(DC), row(DC), row(DC), row(DC), row(DC))
    in_specs = [row(D_MODEL), _layer_spec((1, D_MODEL), layer), _layer_spec((D_MODEL, A_WIDTH), layer),
                _layer_spec((1, LANES), layer), _layer_spec((SUBLANES, DC), layer), _const_spec((tm, tm)),
                _const_spec(place.shape)]
    args = [x, w["g1"], w["w_a"], w["fbias"], w["lbrows"], tri, place]
    aliases = {len(args) + a: out_idx for a, out_idx in enumerate((2, 4, 6))}
    in_specs += [pl.BlockSpec(memory_space=pl.ANY)] * len(stacks)
    args += list(stacks)
    return pl.pallas_call(
        functools.partial(_in_proj_kernel, tiles_per_seq=tiles_per_seq, n_alias=len(aliases),
                          feature_major=feature_major),
        grid=(n // tm,),
        in_specs=in_specs,
        out_specs=out_specs,
        out_shape=out_shape,
        input_output_aliases=aliases,
        scratch_shapes=[pltpu.VMEM((8, LANES), F32)],
        compiler_params=pltpu.CompilerParams(dimension_semantics=("arbitrary",),
                                             vmem_limit_bytes=VMEM_LIMIT),
        name="in_proj",
    )(*args)


def _cumsum_kernel(lf_ref, hi_ref, mid_ref, lo_ref, *, n, pivot):
    x = lf_ref[...]
    lane = lax.broadcasted_iota(jnp.int32, x.shape, 1)
    s = 1
    while s < n:
        x = x + jnp.where(lane >= s, pltpu.roll(x, s, 1), 0.0)
        s *= 2
    piv = jnp.sum(jnp.where(lane == pivot, x, 0.0), axis=1, keepdims=True)
    hi, mid, lo = _split3(LOG2E * (piv - x))
    hi_ref[...] = hi
    mid_ref[...] = mid
    lo_ref[...] = lo


def _cumsum_bias(lf_t, pivot):
    bsz, hh, n = lf_t.shape
    spec = pl.BlockSpec((bsz * hh, n), lambda i: (0, 0))
    sds = jax.ShapeDtypeStruct((bsz * hh, n), BF16)
    outs = pl.pallas_call(
        functools.partial(_cumsum_kernel, n=n, pivot=pivot),
        grid=(1,),
        in_specs=[spec],
        out_specs=(spec, spec, spec),
        out_shape=(sds, sds, sds),
        name="cumsum_bias",
    )(lf_t.reshape(bsz * hh, n))
    return tuple(o.reshape(bsz, hh, n) for o in outs)


HIST_ROWS = 32
HIST_PAD = HIST_ROWS - (CONV_W - 1)


def _conv_kernel(u_ref, hist_ref, w_ref, cb_ref, g_ref, b_ref, feat_ref, new_ref, buf_ref, *, tl, rc):
    t = pl.program_id(1)

    @pl.when(t == 0)
    def _():
        buf_ref[0:HIST_ROWS, :] = hist_ref[0]

    @pl.when(t > 0)
    def _():
        buf_ref[0:HIST_ROWS, :] = buf_ref[tl:tl + HIST_ROWS, :]

    buf_ref[HIST_ROWS:HIST_ROWS + tl, :] = u_ref[0]
    new_ref[0] = buf_ref[tl:tl + HIST_ROWS, :]

    cb = cb_ref[...]
    g = g_ref[...]
    b = b_ref[...]
    for r0 in range(0, tl, rc):
        y = cb
        for r in range(SUBLANES):
            z = None
            for a in range((HIST_ROWS + SUBLANES) // SUBLANES):
                j = SUBLANES * a + r - HIST_PAD
                if 0 <= j < CONV_W:
                    nrows = rc + (SUBLANES if r else 0)
                    term = w_ref[j:j + 1, :] * buf_ref[r0 + SUBLANES * a:r0 + SUBLANES * a + nrows, :]
                    z = term if z is None else z + term
            y = y + z[r:r + rc]
        mu = jnp.mean(y, axis=-1, keepdims=True)
        yc = y - mu
        yn = yc * lax.rsqrt(jnp.mean(yc * yc, axis=-1, keepdims=True) + EPS) * g + b
        feat_ref[0, r0:r0 + rc, :] = _silu(yn).astype(BF16)


def _conv(u, hist, w, layer, tl):
    bsz, L, _ = u.shape
    rc = min(tl, 64)
    vec = _layer_spec((1, CA), layer)
    return pl.pallas_call(
        functools.partial(_conv_kernel, tl=tl, rc=rc),
        grid=(bsz, L // tl),
        in_specs=[pl.BlockSpec((1, tl, CA), lambda i, t: (i, t, 0)),
                  pl.BlockSpec((1, HIST_ROWS, CA), lambda i, t: (i, 0, 0)),
                  _layer_spec((HIST_ROWS, CA), layer), vec, vec, vec],
        out_specs=(pl.BlockSpec((1, tl, CA), lambda i, t: (i, t, 0)),
                   pl.BlockSpec((1, HIST_ROWS, CA), lambda i, t: (i, 0, 0))),
        out_shape=(jax.ShapeDtypeStruct((bsz, L, CA), BF16),
                   jax.ShapeDtypeStruct((bsz, HIST_ROWS, CA), F32)),
        scratch_shapes=[pltpu.VMEM((tl + HIST_ROWS, CA), F32)],
        compiler_params=pltpu.CompilerParams(dimension_semantics=("parallel", "arbitrary")),
        name="conv",
    )(u, hist, w["conv_w"], w["conv_b"], w["ln_g"], w["ln_b"])


ONES_ROWS = 16


def _attn_kernel(q_ref, k_ref, vt_ref, o_ref, *, tq):
    i = pl.program_id(1)
    lane = lax.broadcasted_iota(jnp.int32, (tq, LANES), 1)
    qm = []
    for p in range(HB // 2):
        q = q_ref[:, p * LANES:(p + 1) * LANES]
        for e in range(2):
            qa = jnp.where(lane // DHB == e, q, jnp.zeros_like(q))
            pick = jnp.where((lane >= 3 * e) & (lane < 3 * e + 3), 1.0, 0.0).astype(BF16)
            qm.append(jnp.concatenate([qa, pick], axis=1))
    ones = jnp.ones((ONES_ROWS, tq), BF16)

    def step(j, carry, diag):
        rows = pl.ds(pl.multiple_of(j * tq, tq), tq)
        if diag:
            kk = lax.broadcasted_iota(jnp.int32, (tq, tq), 0)
            qq = lax.broadcasted_iota(jnp.int32, (tq, tq), 1)
            keep = kk <= qq
        ss = []
        for p in range(HB // 2):
            kblk = k_ref[rows, p * PAIR_LANES:(p + 1) * PAIR_LANES]
            for e in range(2):
                ss.append(_dot_nt(kblk, qm[2 * p + e]))
        if diag:
            ss = [jnp.where(keep, s, NEG_BIG) for s in ss]
        ms = [jnp.maximum(carry[h][0], jnp.max(ss[h], axis=0, keepdims=True)) for h in range(HB)]
        out = []
        for h in range(HB):
            m, acc = carry[h]
            alpha = jnp.exp2(m - ms[h])
            p = jnp.exp2(ss[h] - ms[h]).astype(BF16)
            vt = jnp.concatenate([vt_ref[j, h * DHB:(h + 1) * DHB, :], ones], axis=0)
            out.append((ms[h], acc * alpha + _dot(vt, p)))
        return tuple(out)

    init = tuple((jnp.full((1, tq), NEG_BIG, F32), jnp.zeros((DHB + ONES_ROWS, tq), F32))
                 for _ in range(HB))
    carry = lax.fori_loop(0, i // 2, lambda jj, c: step(2 * jj + 1, step(2 * jj, c, False), False), init)

    def finish(carry):
        o_t = jnp.concatenate([acc[:DHB] / acc[DHB:DHB + 1] for _, acc in carry], axis=0)
        o_ref[...] = o_t.T.astype(BF16)

    @pl.when(i % 2 == 0)
    def _():
        finish(step(i, carry, True))

    @pl.when(i % 2 == 1)
    def _():
        finish(step(i, step(i - 1, carry, False), True))


def _attn_prompt(qs, kaug, vt, bsz, S, tq):
    nq = S // tq
    once = pl.Buffered(1)
    return pl.pallas_call(
        functools.partial(_attn_kernel, tq=tq),
        grid=(bsz, nq),
        in_specs=[pl.BlockSpec((tq, DB), lambda b, i: (b * nq + i, 0)),
                  pl.BlockSpec((S, KAUG_WIDTH), lambda b, i: (b, 0), pipeline_mode=once),
                  pl.BlockSpec((nq, DB, tq), lambda b, i: (b, 0, 0), pipeline_mode=once)],
        out_specs=pl.BlockSpec((tq, DB), lambda b, i: (b * nq + i, 0)),
        out_shape=jax.ShapeDtypeStruct((bsz * S, DB), BF16),
        compiler_params=pltpu.CompilerParams(dimension_semantics=("parallel", "arbitrary"),
                                             vmem_limit_bytes=VMEM_LIMIT),
        name="attn_prompt",
    )(qs, kaug, vt)


def _attn_cached_kernel(q_ref, kc_ref, vc_ref, kn_ref, vn_ref, bhi_ref, bmid_ref, blo_ref, o_ref, *, L, P):
    q = q_ref[0]
    lane_head = lax.broadcasted_iota(jnp.int32, (L, DB), 1) // DHB
    q_bd = jnp.concatenate([jnp.where(lane_head == h, q, jnp.zeros_like(q)) for h in range(HB)], axis=0)
    kc = kc_ref[...].astype(BF16)
    vc = vc_ref[...].astype(BF16)
    zpad = jnp.zeros((LANES - L, DB), BF16)
    kn = jnp.concatenate([kn_ref[0].astype(BF16), zpad], axis=0)
    vn = jnp.concatenate([vn_ref[0].astype(BF16), zpad], axis=0)
    bias = (bhi_ref[0].astype(F32) + bmid_ref[0].astype(F32)) + blo_ref[0].astype(F32)

    s_c = _dot(q_bd, kc)
    s_n = _dot_nt(q_bd, kn)
    tt = lax.broadcasted_iota(jnp.int32, (L, LANES), 0)
    uu = lax.broadcasted_iota(jnp.int32, (L, LANES), 1)
    causal = uu <= tt
    pcs, pns, ls = [], [], []
    for h in range(HB):
        sc = s_c[h * L:(h + 1) * L, :] + bias[h:h + 1, :P]
        sn = jnp.where(causal, s_n[h * L:(h + 1) * L, :] + bias[h:h + 1, P:], NEG_BIG)
        m = jnp.maximum(jnp.max(sc, axis=1, keepdims=True), jnp.max(sn, axis=1, keepdims=True))
        pc = jnp.exp2(sc - m)
        pn = jnp.exp2(sn - m)
        ls.append(jnp.sum(pc, axis=1, keepdims=True) + jnp.sum(pn, axis=1, keepdims=True))
        pcs.append(pc.astype(BF16))
        pns.append(pn.astype(BF16))
    o_all = _dot_nt(jnp.concatenate(pcs, axis=0), vc) + _dot(jnp.concatenate(pns, axis=0), vn)
    out = jnp.zeros((L, DB), F32)
    for h in range(HB):
        out = out + jnp.where(lane_head == h, o_all[h * L:(h + 1) * L, :] / ls[h], 0.0)
    o_ref[0] = out.astype(BF16)


def _attn_cached(qs, k_cache_t, v_cache_t, layer, k_new, v_new, bhi, bmid, blo):
    bsz, L, _ = qs.shape
    P = k_cache_t.shape[-1]
    seq = lambda r, w: pl.BlockSpec((1, r, w), lambda b: (b, 0, 0))
    cache = pl.BlockSpec((None, None, DB, P), lambda b: (layer, b, 0, 0))
    return pl.pallas_call(
        functools.partial(_attn_cached_kernel, L=L, P=P),
        grid=(bsz,),
        in_specs=[seq(L, DB), cache, cache, seq(L, DB), seq(L, DB),
                  seq(HB, P + LANES), seq(HB, P + LANES), seq(HB, P + LANES)],
        out_specs=seq(L, DB),
        out_shape=jax.ShapeDtypeStruct((bsz, L, DB), BF16),
        compiler_params=pltpu.CompilerParams(dimension_semantics=("parallel",),
                                             vmem_limit_bytes=VMEM_LIMIT),
        name="attn_cached",
    )(qs, k_cache_t, v_cache_t, k_new, v_new, bhi, bmid, blo)


def _hgrn_exponent_matrix(C):
    nl = int(math.log2(C))
    rows = []
    idx = np.arange(C)
    for lv in range(nl):
        sz = 1 << lv
        m = np.zeros((C, C), np.float32)
        for t in range(C):
            bnd = ((t >> (lv + 1)) << (lv + 1)) + sz - 1
            if (t >> lv) & 1:
                m[t, (idx > bnd) & (idx <= t)] = 1.0
            else:
                m[t, (idx > t) & (idx <= bnd)] = 1.0
        rows.append(m)
    rows.append((idx[None, :] <= idx[:, None]).astype(np.float32))
    rows.append((idx[None, :] > idx[:, None]).astype(np.float32))
    p = np.concatenate(rows, axis=0)
    return np.concatenate([p, p], axis=1)


def _hgrn_kernel(p_ref, qh_ref, lf_ref, kk_ref, vv_ref, og_ref, s0_ref, g_ref, hg_ref, sout_ref, st_ref,
                 *, C, T):
    t = pl.program_id(1)
    nl = int(math.log2(C))

    @pl.when(t == 0)
    def _():
        for h in range(HC):
            st_ref[h] = s0_ref[0, h].T

    row = lax.broadcasted_iota(jnp.int32, (C, C), 0)
    col = lax.broadcasted_iota(jnp.int32, (C, C), 1)
    diff = row ^ col
    masks = [(jnp.right_shift(diff, lv) == 1) & ((jnp.right_shift(row, lv) & 1) == 1) for lv in range(nl)]
    diag = row == col
    g = g_ref[...]
    pmat = p_ref[...]

    pre = {}
    for c0 in range(0, T, C):
        lf = LOG2E * lf_ref[0, c0:c0 + C, :]
        hi = lf.astype(BF16)
        lo = (lf - hi.astype(F32)).astype(BF16)
        e_all = jnp.exp2(_dot(pmat, jnp.concatenate([hi, lo], axis=0)))
        for h in range(HC):
            hs = slice(h * DK, (h + 1) * DK)
            qb = qh_ref[0, c0:c0 + C, hs]
            kb = kk_ref[0, c0:c0 + C, hs]
            q = qb.astype(F32)
            k = kb.astype(F32)
            k_t = k.T
            a = jnp.where(diag, _dot(qb, k_t.astype(BF16)), 0.0)
            for lv in range(nl):
                e = e_all[lv * C:(lv + 1) * C, hs]
                a = jnp.where(masks[lv], _dot((q * e).astype(BF16), (k_t * e.T).astype(BF16)), a)
            e_q = e_all[nl * C:(nl + 1) * C, hs]
            e_k = e_all[(nl + 1) * C:(nl + 2) * C, hs]
            pre[c0, h] = (a.astype(BF16), (q * e_q).astype(BF16), (k * e_k).astype(BF16), e_q[C - 1:C, :])

    for h in range(HC):
        hs = slice(h * DK, (h + 1) * DK)
        st = st_ref[h]
        for c0 in range(0, T, C):
            a, q_dec, k_dec, chunk_dec = pre[c0, h]
            v = vv_ref[0, c0:c0 + C, hs]
            o = _dot(a, v) + _dot_nt(q_dec, st.astype(BF16))
            st = st * chunk_dec + _dot_tn(v, k_dec)
            o = o * lax.rsqrt(jnp.mean(o * o, axis=-1, keepdims=True) + EPS) * g
            hg_ref[0, c0:c0 + C, hs] = (o * og_ref[0, c0:c0 + C, hs].astype(F32)).astype(BF16)
        st_ref[h] = st

    @pl.when(t == pl.num_programs(1) - 1)
    def _():
        for h in range(HC):
            sout_ref[0, h] = st_ref[h].T


def _hgrn(qh, lfc, kk, vv, og, s0, g, layer, C, T):
    bsz, L, _ = qh.shape
    pmat = jnp.asarray(_hgrn_exponent_matrix(C), BF16)
    tile = pl.BlockSpec((1, T, DC), lambda b, t: (b, t, 0))
    state = pl.BlockSpec((1, HC, DK, DV), lambda b, t: (b, 0, 0, 0))
    return pl.pallas_call(
        functools.partial(_hgrn_kernel, C=C, T=T),
        grid=(bsz, L // T),
        in_specs=[_const_spec(pmat.shape), tile, tile, tile, tile, tile, state, _layer_spec((1, DV), layer)],
        out_specs=(tile, state),
        out_shape=(jax.ShapeDtypeStruct((bsz, L, DC), BF16),
                   jax.ShapeDtypeStruct((bsz, HC, DK, DV), F32)),
        scratch_shapes=[pltpu.VMEM((HC, DV, DK), F32)],
        compiler_params=pltpu.CompilerParams(dimension_semantics=("parallel", "arbitrary")),
        name="hgrn",
    )(pmat, qh, lfc, kk, vv, og, s0, g)


FF_CHUNK = 1024


def _merge_ffn_kernel(x_ref, fa_ref, fb_ref, fc_ref, g1_ref, wg_ref, wa_ref, wb_ref, wc_ref, wo_ref,
                      g2_ref, wup_ref, wdn_ref, gf_ref, o_ref, *, final):
    x = x_ref[...]
    h1 = _rms(x, g1_ref[...]).astype(BF16)
    m = None
    for i, (f_ref, w_ref) in enumerate(((fa_ref, wa_ref), (fb_ref, wb_ref), (fc_ref, wc_ref))):
        gate = _sigmoid(_dot(h1, wg_ref[:, i * D_MODEL:(i + 1) * D_MODEL]))
        y = gate * _dot(f_ref[...], w_ref[...])
        m = y if m is None else m + y
    x = x + _dot(m.astype(BF16), wo_ref[...])
    h2 = _rms(x, g2_ref[...]).astype(BF16)
    acc = None
    for c in range(0, D_FF, FF_CHUNK):
        up = jnp.maximum(_dot(h2, wup_ref[:, c:c + FF_CHUNK]), 0.0)
        d = _dot((up * up).astype(BF16), wdn_ref[c:c + FF_CHUNK, :])
        acc = d if acc is None else acc + d
    x = x + acc
    if final:
        x = _rms(x, gf_ref[...])
    o_ref[...] = x


def _merge_ffn(x, fa, fb, fc, w, layer, final_g, tm, final):
    n = x.shape[0]
    row = lambda w_: pl.BlockSpec((tm, w_), lambda i: (i, 0))
    lw = lambda *shape: _layer_spec(shape, layer)
    return pl.pallas_call(
        functools.partial(_merge_ffn_kernel, final=final),
        grid=(n // tm,),
        in_specs=[row(D_MODEL), row(CA), row(DB), row(DC),
                  lw(1, D_MODEL), lw(D_MODEL, 3 * D_MODEL),
                  lw(CA, D_MODEL), lw(DB, D_MODEL), lw(DC, D_MODEL),
                  lw(D_MODEL, D_MODEL), lw(1, D_MODEL),
                  lw(D_MODEL, D_FF), lw(D_FF, D_MODEL), _const_spec((1, D_MODEL))],
        out_specs=row(D_MODEL),
        out_shape=jax.ShapeDtypeStruct((n, D_MODEL), F32),
        compiler_params=pltpu.CompilerParams(dimension_semantics=("parallel",),
                                             vmem_limit_bytes=VMEM_LIMIT),
        name="merge_ffn",
    )(x, fa, fb, fc, w["g1"], w["w_gate"], w["w_a_out"], w["w_b_out"], w["w_c_out"], w["w_o"],
      w["g2"], w["w_up"], w["w_down"], final_g)


def _stacked_weights(p, lbs):
    w_in = p["w_in"]
    cols = lambda a, b: w_in[:, :, a:b]
    w_a = jnp.concatenate([cols(O_AV, O_QB), cols(O_QB, O_FB), cols(O_QC, O_GA),
                           jnp.pad(cols(O_FB, O_QC), ((0, 0), (0, 0), (0, LANES - HB)))], axis=2).astype(BF16)
    row = lambda v: v[:, None, :].astype(F32)
    lbrows = jnp.concatenate([jnp.log(lbs)[:, None], jnp.log1p(-lbs)[:, None], (1.0 - lbs)[:, None],
                              jnp.zeros((lbs.shape[0], 5, HC * DK), F32)], axis=1)
    return dict(
        w_a=w_a, g1=row(p["norm1_g"]),
        fbias=row(jnp.pad(p["fox_bf"], ((0, 0), (0, LANES - HB)))),
        lbrows=lbrows,
        conv_w=jnp.pad(p["conv_w"], ((0, 0), (0, HIST_ROWS - CONV_W), (0, 0))),
        conv_b=row(p["conv_b"]), ln_g=row(p["conv_ln_g"]), ln_b=row(p["conv_ln_b"]),
        hn_g=row(p["hgrn_norm_g"]),
        w_gate=cols(O_GA, O_END).astype(BF16),
        w_a_out=p["w_a_out"].astype(BF16), w_b_out=p["w_b_out"].astype(BF16),
        w_c_out=p["w_c_out"].astype(BF16), w_o=p["w_o"].astype(BF16),
        g2=row(p["norm2_g"]), w_up=p["w_up"].astype(BF16), w_down=p["w_down"].astype(BF16),
    )


def _pad_hist(h):
    return jnp.pad(h, ((0, 0), (HIST_PAD, 0), (0, 0)))


def _trunk_layer(x, w, layer, depth, stacks, final_g, final, conv_hist, s0, cache,
                 *, tm_a, tm_d, tl, tq, hg_tile):
    bsz, L, _ = x.shape
    n = bsz * L
    xf = x.reshape(n, D_MODEL)
    u, qs, k_all, kaug, v_all, vt, lf_all, qh, lfc, kk, vv, og = _in_proj(
        xf, w, layer, depth, stacks, tm_a, max(L // tm_a, 1), feature_major=cache is None)
    seq = lambda a: a.reshape(bsz, L, a.shape[-1])

    feat, conv_new = _conv(seq(u), _pad_hist(conv_hist), w, layer, tl)

    if cache is None:
        o_b = _attn_prompt(qs, kaug, vt, bsz, L, tq)
    else:
        k_cache_t, v_cache_t, lf_cache = cache
        P = k_cache_t.shape[-1]
        lf_t = seq(lf_all[layer]).transpose(0, 2, 1)
        lf_cat = jnp.concatenate([lf_cache.transpose(0, 2, 1), lf_t,
                                  jnp.zeros((bsz, HB, LANES - L), F32)], axis=-1)
        hi, mid, lo = _cumsum_bias(lf_cat, pivot=P - 1)
        o_b = _attn_cached(seq(qs), k_cache_t, v_cache_t, layer,
                           seq(k_all[layer]), seq(v_all[layer]), hi, mid, lo).reshape(n, DB)

    C = min(HGRN_CHUNK, L)
    hg, s_new = _hgrn(seq(qh), seq(lfc), seq(kk), seq(vv), seq(og), s0, w["hn_g"], layer, C, hg_tile)

    x_new = _merge_ffn(xf, feat.reshape(n, CA), o_b, hg.reshape(n, DC), w, layer, final_g, tm_d, final)
    return x_new.reshape(bsz, L, D_MODEL), (k_all, v_all, lf_all), conv_new[:, HIST_PAD:], s_new


def _lower_bounds(p):
    s = jax.nn.softmax(p.astype(F32), axis=0)
    return jnp.maximum(jnp.cumsum(s, axis=0) - s[0], 0.0)


TOKEN_TILE = 512
HGRN_TILE = 1024
SAMPLE_FFN_TILE = 256


def _tile_plan(bsz, seq, prompt):
    n = bsz * seq
    if prompt:
        t = min(TOKEN_TILE, seq)
        return dict(tm_a=t, tm_d=min(TOKEN_TILE, n), tl=t, tq=t, hg_tile=min(HGRN_TILE, seq))
    return dict(tm_a=min(TOKEN_TILE, n), tm_d=min(SAMPLE_FFN_TILE, n), tl=seq, tq=None, hg_tile=seq)


def kernel(x_prompt, x_sample, cache_fox_k, cache_fox_v, cache_fox_logf, state_conv, state_hgrn,
           norm1_g, w_in, conv_w, conv_b, conv_ln_g, conv_ln_b, w_a_out, fox_bf, w_b_out,
           hgrn_lb_param, hgrn_norm_g, w_c_out, w_o, norm2_g, w_up, w_down, final_g):
    p = dict(norm1_g=norm1_g, w_in=w_in, conv_w=conv_w, conv_b=conv_b, conv_ln_g=conv_ln_g,
             conv_ln_b=conv_ln_b, w_a_out=w_a_out, fox_bf=fox_bf, w_b_out=w_b_out,
             hgrn_norm_g=hgrn_norm_g, w_c_out=w_c_out, w_o=w_o, norm2_g=norm2_g, w_up=w_up, w_down=w_down)
    depth = w_in.shape[0]
    lbs = _lower_bounds(hgrn_lb_param)
    fg = final_g.reshape(1, D_MODEL).astype(F32)
    xp, xs = x_prompt, x_sample
    bp, sp, _ = xp.shape
    bs, ls, _ = xs.shape
    zero_hist = jnp.zeros((bp, CONV_W - 1, CA), F32)
    zero_s = jnp.zeros((bp, HC, DK, DV), F32)
    past = cache_fox_k.shape[2]
    k_cache_t = cache_fox_k.transpose(0, 1, 3, 4, 2).reshape(depth, bs, DB, past)
    v_cache_t = cache_fox_v.transpose(0, 1, 3, 4, 2).reshape(depth, bs, DB, past)
    w = _stacked_weights(p, lbs)
    stacks_p = (jnp.zeros((depth, bp, DB, sp), F32), jnp.zeros((depth, bp, DB, sp), F32),
                jnp.zeros((depth, bp * sp, HB), F32))
    stacks_s = (jnp.zeros((depth, bs * ls, DB), F32), jnp.zeros((depth, bs * ls, DB), F32),
                jnp.zeros((depth, bs * ls, HB), F32))
    conv_p, conv_s, hgrn_p, hgrn_s = [], [], [], []
    for l in range(depth):
        final = l == depth - 1
        xp, stacks_p, cp, hp = _trunk_layer(
            xp, w, l, depth, stacks_p, fg, final, zero_hist, zero_s, None, **_tile_plan(bp, sp, True))
        xs, stacks_s, cs, hs = _trunk_layer(
            xs, w, l, depth, stacks_s, fg, final, state_conv[l], state_hgrn[l],
            (k_cache_t, v_cache_t, cache_fox_logf[l]), **_tile_plan(bs, ls, False))
        conv_p.append(cp); conv_s.append(cs); hgrn_p.append(hp); hgrn_s.append(hs)
    kp, vp, lfp = stacks_p
    ks, vs, lfs = stacks_s
    heads_last = lambda a: a.reshape(depth, bp, HB, DHB, sp).transpose(0, 1, 4, 2, 3)
    return (xp, xs,
            heads_last(kp), heads_last(vp),
            lfp.reshape(depth, bp, sp, HB), jnp.stack(conv_p), jnp.stack(hgrn_p),
            ks.reshape(depth, bs, ls, HB, DHB), vs.reshape(depth, bs, ls, HB, DHB),
            lfs.reshape(depth, bs, ls, HB), jnp.stack(conv_s), jnp.stack(hgrn_s))
```
